```python
import jax, jax.numpy as jnp
from jax import lax
import numpy as np

D_MODEL = 1024
BATCH = 8
SEQ = 2048
DEPTH = 4
DEC_BATCH = 128
DEC_SEQ = 8
PAST_LEN = 2048
PAGE_SIZE = 128

N_META = 16
EPS = 1e-6
MASK_NEG = -1e30
A_WIDTH = D_MODEL // 2
A_HEAD = 128
A_HEADS = A_WIDTH // A_HEAD
A_CHUNK = 64
B_WIDTH = D_MODEL // 2
CONV_W = 31
C_HEADS = 8
C_HEAD_DIM = 64
C_WIDTH = C_HEADS * C_HEAD_DIM
C_KV_HEADS = 2
C_KV_WIDTH = C_KV_HEADS * C_HEAD_DIM
IDX_HEADS = 4
IDX_DIM = 64
TOPK_MAX = 256
Q_BLOCK = 128
ROPE_THETA = 500000.0
ROT_DIM = C_HEAD_DIM // 4
IDX_SCALE = (IDX_HEADS * IDX_DIM) ** -0.5
IN_SIZES = (A_WIDTH, A_WIDTH, A_WIDTH, A_WIDTH, 2 * B_WIDTH, B_WIDTH, C_WIDTH, C_KV_WIDTH, C_KV_WIDTH, IDX_HEADS * IDX_DIM, IDX_DIM, IDX_HEADS, C_WIDTH, 3 * D_MODEL)
N_IN = sum(IN_SIZES)
F32 = jnp.float32

kernel_name = 'hybrid_hgrn2_conformer_dsa_step'


def rmsnorm(x, g):
    xf = x.astype(F32)
    y = xf * lax.rsqrt(jnp.mean(xf * xf, axis=-1, keepdims=True) + EPS)
    return (y * g.astype(F32)).astype(x.dtype)


def layernorm(x, g, b):
    xf = x.astype(F32)
    mu = jnp.mean(xf, axis=-1, keepdims=True)
    d = xf - mu
    var = jnp.mean(d * d, axis=-1, keepdims=True)
    return (d * lax.rsqrt(var + EPS) * g.astype(F32) + b.astype(F32)).astype(x.dtype)


def rope(x, pos):
    half = ROT_DIM // 2
    inv = ROPE_THETA ** (-jnp.arange(half, dtype=F32) * 2.0 / ROT_DIM)
    ang = pos.astype(F32)[:, None] * inv
    cos = jnp.cos(ang)[:, None, :]
    sin = jnp.sin(ang)[:, None, :]
    xf = x.astype(F32)
    x1 = xf[..., :half]
    x2 = xf[..., half:ROT_DIM]
    out = jnp.concatenate([x1 * cos - x2 * sin, x2 * cos + x1 * sin, xf[..., ROT_DIM:]], axis=-1)
    return out.astype(x.dtype)


def project(h, norm_g, w_in, b_in):
    u = rmsnorm(h, norm_g) @ w_in + b_in
    parts, start = [], 0
    for n in IN_SIZES:
        parts.append(u[..., start:start + n])
        start += n
    return parts


def hgrn_inputs(aq, af, ai, lb):
    shp = aq.shape[:-1] + (A_HEADS, A_HEAD)
    f = af.astype(F32)
    lb = lb.astype(F32)
    logf = jnp.log(lb + (1.0 - lb) * jax.nn.sigmoid(f))
    k = (1.0 - lb) * jax.nn.sigmoid(-f)
    q = jax.nn.silu(aq.astype(F32))
    return q.reshape(shp), k.reshape(shp), ai.astype(F32).reshape(shp), logf.reshape(shp)


def gla_chunk(S, q, k, v, logf):
    c = q.shape[1]
    b = jnp.cumsum(logf, axis=1)
    o_inter = jnp.einsum('bthk,bhkv->bthv', q * jnp.exp(b), S)
    causal = jnp.tril(jnp.ones((c, c), dtype=bool))[None, :, :, None, None]
    diff = b[:, :, None] - b[:, None, :]
    dec = jnp.exp(jnp.where(causal, diff, MASK_NEG))
    att = jnp.einsum('bthk,btshk,bshk->bths', q, dec, k)
    o_intra = jnp.einsum('bths,bshv->bthv', att, v)
    blast = b[:, -1]
    k_dec = k * jnp.exp(blast[:, None] - b)
    S_new = jnp.exp(blast)[..., None] * S + jnp.einsum('bchk,bchv->bhkv', k_dec, v)
    return S_new, o_inter + o_intra


def hgrn_prompt(q, k, v, logf):
    nb = q.shape[0]
    S0 = jnp.zeros((nb, A_HEADS, A_HEAD, A_HEAD), F32)
    S1, o_meta = gla_chunk(S0, q[:, :N_META], k[:, :N_META], v[:, :N_META], logf[:, :N_META])

    def to_chunks(a):
        a = a[:, N_META:]
        n = a.shape[1] // A_CHUNK
        return a.reshape((nb, n, A_CHUNK) + a.shape[2:]).swapaxes(0, 1)

    def step(S, xs):
        return gla_chunk(S, *xs)

    S_fin, o = lax.scan(step, S1, (to_chunks(q), to_chunks(k), to_chunks(v), to_chunks(logf)))
    o = o.swapaxes(0, 1).reshape(nb, -1, A_HEADS, A_HEAD)
    return jnp.concatenate([o_meta, o], axis=1), S_fin


def conformer_conv(glu_in, z, buf, conv_w, conv_b, ln_g, ln_b, conv_pw, w_pb):
    a, g = jnp.split(glu_in, 2, axis=-1)
    u = a * jax.nn.sigmoid(g)
    xp = jnp.concatenate([buf.astype(u.dtype), u], axis=1)
    y = lax.conv_general_dilated(xp, conv_w[:, None, :].astype(xp.dtype), window_strides=(1,), padding='VALID', dimension_numbers=('NWC', 'WIO', 'NWC'), feature_group_count=B_WIDTH) + conv_b
    y = jax.nn.silu(layernorm(y, ln_g, ln_b))
    out = ((y @ conv_pw) * jax.nn.silu(z)) @ w_pb
    return out, xp[:, xp.shape[1] - (CONV_W - 1):]


def index_scores(qi, wi, ki):
    s = jnp.einsum('bqhd,bsd->bqhs', qi.astype(F32), ki.astype(F32))
    return jnp.einsum('bqhs,bqh->bqs', jax.nn.relu(s), wi.astype(F32) * IDX_SCALE)


def select_keys(scores, allowed, n_sel):
    scores = jnp.where(allowed, scores, MASK_NEG)
    val, idx = lax.top_k(scores, n_sel)
    return idx, val > 0.5 * MASK_NEG


def gather_rows(x, idx):
    return jax.vmap(lambda xb, ib: xb[ib])(x, idx)


def sparse_attend(q, kg, vg, valid):
    nb, nq = q.shape[:2]
    qg = q.reshape(nb, nq, C_KV_HEADS, C_HEADS // C_KV_HEADS, C_HEAD_DIM).astype(F32)
    logits = jnp.einsum('bqhgd,bqnhd->bqhgn', qg, kg.astype(F32)) * (C_HEAD_DIM ** -0.5)
    logits = jnp.where(valid[:, :, None, None, :], logits, MASK_NEG)
    p = jax.nn.softmax(logits, axis=-1)
    o = jnp.einsum('bqhgn,bqnhd->bqhgd', p, vg.astype(F32))
    return o.reshape(nb, nq, C_WIDTH).astype(q.dtype)


def dsa_prompt(q, k, v, qi, ki, wi):
    nb, nt = q.shape[:2]
    n_sel = min(TOPK_MAX, nt // 4)
    n_blk = -(-nt // Q_BLOCK)
    tp = n_blk * Q_BLOCK

    def pad(a):
        return jnp.pad(a, [(0, 0), (0, tp - nt)] + [(0, 0)] * (a.ndim - 2))

    qP, qiP, wiP = pad(q), pad(qi), pad(wi)
    key_pos = jnp.arange(nt)

    def block(i):
        start = i * Q_BLOCK
        qb = lax.dynamic_slice_in_dim(qP, start, Q_BLOCK, axis=1)
        qib = lax.dynamic_slice_in_dim(qiP, start, Q_BLOCK, axis=1)
        wib = lax.dynamic_slice_in_dim(wiP, start, Q_BLOCK, axis=1)
        qpos = start + jnp.arange(Q_BLOCK)
        sc = index_scores(qib, wib, ki)
        idx, valid = select_keys(sc, key_pos[None, :] <= qpos[:, None], n_sel)
        return sparse_attend(qb, gather_rows(k, idx), gather_rows(v, idx), valid)

    out = lax.map(block, jnp.arange(n_blk))
    return out.transpose(1, 0, 2, 3).reshape(nb, tp, C_WIDTH)[:, :nt]


def dsa_sample(q, k_new, v_new, qi, ki_new, wi, cache_k, cache_v, cache_kidx, page_table):
    nb, ns = q.shape[:2]
    past = page_table.shape[1] * PAGE_SIZE
    n_keys = past + ns
    n_sel = min(TOPK_MAX, n_keys // 4)
    ki_past = cache_kidx[page_table].reshape(nb, past, IDX_DIM)
    ki_all = jnp.concatenate([ki_past.astype(ki_new.dtype), ki_new], axis=1)
    sc = index_scores(qi, wi, ki_all)
    allowed = jnp.arange(n_keys)[None, :] <= (past + jnp.arange(ns))[:, None]
    idx, valid = select_keys(sc, allowed, n_sel)
    is_past = idx < past
    pidx = jnp.minimum(idx, past - 1)
    phys = jax.vmap(lambda pt, ii: pt[ii])(page_table, pidx // PAGE_SIZE) * PAGE_SIZE + pidx % PAGE_SIZE
    nidx = jnp.clip(idx - past, 0, ns - 1)

    def fetch(pool, new):
        flat = pool.reshape((-1,) + pool.shape[2:])
        return jnp.where(is_past[..., None, None], flat[phys].astype(new.dtype), gather_rows(new, nidx))

    return sparse_attend(q, fetch(cache_k, k_new), fetch(cache_v, v_new), valid)


def layer_forward(h, pos, lw, past):
    (norm_g, w_in, b_in, lb, gn_g, conv_w, conv_b, ln_g, ln_b, conv_pw, w_pa, w_pb, w_pc, w_out) = lw
    (aq, af, ai, az, b_glu, bz, cq, ck, cv, cqi, cki, cw, cz, gate) = project(h, norm_g, w_in, b_in)
    nb, nt = h.shape[:2]
    qa, ka, va, logf = hgrn_inputs(aq, af, ai, lb)
    if past is None:
        oa, s_new = hgrn_prompt(qa, ka, va, logf)
        buf = jnp.zeros((nb, CONV_W - 1, B_WIDTH), h.dtype)
    else:
        cache_k, cache_v, cache_kidx, s_old, buf, page_table = past
        s_new, oa = gla_chunk(s_old.astype(F32), qa, ka, va, logf)
        s_new = s_new.astype(s_old.dtype)
    ya = (rmsnorm(oa, gn_g).reshape(nb, nt, A_WIDTH).astype(h.dtype) * jax.nn.silu(az)) @ w_pa
    yb, buf_new = conformer_conv(b_glu, bz, buf, conv_w, conv_b, ln_g, ln_b, conv_pw, w_pb)
    qc = rope(cq.reshape(nb, nt, C_HEADS, C_HEAD_DIM), pos)
    kc = rope(ck.reshape(nb, nt, C_KV_HEADS, C_HEAD_DIM), pos)
    vc = cv.reshape(nb, nt, C_KV_HEADS, C_HEAD_DIM)
    qi = rope(cqi.reshape(nb, nt, IDX_HEADS, IDX_DIM), pos)
    ki = rope(cki[:, :, None, :], pos)[:, :, 0]
    if past is None:
        oc = dsa_prompt(qc, kc, vc, qi, ki, cw)
    else:
        oc = dsa_sample(qc, kc, vc, qi, ki, cw, cache_k, cache_v, cache_kidx, page_table)
    yc = (oc * jax.nn.silu(cz)) @ w_pc
    ga, gb, gc = jnp.split(jax.nn.sigmoid(gate), 3, axis=-1)
    h = h + (ga * ya + gb * yb + gc * yc) @ w_out
    return h, (kc, vc, ki, s_new, buf_new)


def setup_inputs(seed: int = 0) -> dict:
    key = jax.random.key(seed)
    ks = jax.random.split(key, 32)
    n_pages = PAST_LEN // PAGE_SIZE
    used = DEC_BATCH * n_pages
    n_phys = used + max(1, used // 4)
    nrm = jax.random.normal
    page_table = jax.random.permutation(ks[0], n_phys)[:used].reshape(DEC_BATCH, n_pages).astype(jnp.int32)
    return {
        'x_prompt': nrm(ks[1], (BATCH, SEQ, D_MODEL), F32),
        'x_sample': nrm(ks[2], (DEC_BATCH, DEC_SEQ, D_MODEL), F32),
        'cache_k': nrm(ks[3], (DEPTH, n_phys, PAGE_SIZE, C_KV_HEADS, C_HEAD_DIM), F32),
        'cache_v': nrm(ks[4], (DEPTH, n_phys, PAGE_SIZE, C_KV_HEADS, C_HEAD_DIM), F32),
        'cache_kidx': nrm(ks[5], (DEPTH, n_phys, PAGE_SIZE, IDX_DIM), F32),
        'state_hgrn': 0.5 * nrm(ks[6], (DEPTH, DEC_BATCH, A_HEADS, A_HEAD, A_HEAD), F32),
        'state_conv': 0.5 * nrm(ks[7], (DEPTH, DEC_BATCH, CONV_W - 1, B_WIDTH), F32),
        'page_table': page_table,
        'meta_tokens': nrm(ks[8], (N_META, D_MODEL), F32),
        'norm_g': 1.0 + 0.05 * nrm(ks[9], (DEPTH, D_MODEL), F32),
        'w_in': nrm(ks[10], (DEPTH, D_MODEL, N_IN), F32) * D_MODEL ** -0.5,
        'b_in': 0.01 * nrm(ks[11], (DEPTH, N_IN), F32),
        'lb_logits': 0.5 * nrm(ks[12], (DEPTH, A_WIDTH), F32),
        'hgrn_norm_g': 1.0 + 0.05 * nrm(ks[13], (DEPTH, A_HEAD), F32),
        'conv_w': nrm(ks[14], (DEPTH, CONV_W, B_WIDTH), F32) * CONV_W ** -0.5,
        'conv_b': 0.01 * nrm(ks[15], (DEPTH, B_WIDTH), F32),
        'conv_ln_g': 1.0 + 0.05 * nrm(ks[16], (DEPTH, B_WIDTH), F32),
        'conv_ln_b': 0.01 * nrm(ks[17], (DEPTH, B_WIDTH), F32),
        'conv_pw': nrm(ks[18], (DEPTH, B_WIDTH, B_WIDTH), F32) * B_WIDTH ** -0.5,
        'w_pa': nrm(ks[19], (DEPTH, A_WIDTH, D_MODEL), F32) * A_WIDTH ** -0.5,
        'w_pb': nrm(ks[20], (DEPTH, B_WIDTH, D_MODEL), F32) * B_WIDTH ** -0.5,
        'w_pc': nrm(ks[21], (DEPTH, C_WIDTH, D_MODEL), F32) * C_WIDTH ** -0.5,
        'w_out': nrm(ks[22], (DEPTH, D_MODEL, D_MODEL), F32) * D_MODEL ** -0.5,
        'final_norm_g': 1.0 + 0.05 * nrm(ks[23], (D_MODEL,), F32),
    }


def reference(x_prompt, x_sample, cache_k, cache_v, cache_kidx, state_hgrn, state_conv, page_table, meta_tokens, norm_g, w_in, b_in, lb_logits, hgrn_norm_g, conv_w, conv_b, conv_ln_g, conv_ln_b, conv_pw, w_pa, w_pb, w_pc, w_out, final_norm_g):
    sm = jax.nn.softmax(lb_logits.astype(F32), axis=0)
    lb_all = jnp.cumsum(sm, axis=0) - sm[0]
    nbp = x_prompt.shape[0]
    meta = jnp.broadcast_to(meta_tokens[None].astype(x_prompt.dtype), (nbp, N_META, x_prompt.shape[2]))
    hp = jnp.concatenate([meta, x_prompt], axis=1)
    pos_p = jnp.arange(hp.shape[1])
    past = page_table.shape[1] * PAGE_SIZE
    pos_s = past + jnp.arange(x_sample.shape[1])
    hs = x_sample
    pk, pv, pki, ps, pc = [], [], [], [], []
    sk, sv, ski, ss, sc = [], [], [], [], []
    for l in range(DEPTH):
        lw = (norm_g[l], w_in[l], b_in[l], lb_all[l], hgrn_norm_g[l], conv_w[l], conv_b[l], conv_ln_g[l], conv_ln_b[l], conv_pw[l], w_pa[l], w_pb[l], w_pc[l], w_out[l])
        hp, (k1, v1, ki1, s1, c1) = layer_forward(hp, pos_p, lw, None)
        hs, (k2, v2, ki2, s2, c2) = layer_forward(hs, pos_s, lw, (cache_k[l], cache_v[l], cache_kidx[l], state_hgrn[l], state_conv[l], page_table))
        pk.append(k1); pv.append(v1); pki.append(ki1); ps.append(s1); pc.append(c1)
        sk.append(k2); sv.append(v2); ski.append(ki2); ss.append(s2); sc.append(c2)
    y_prompt = rmsnorm(hp, final_norm_g)[:, N_META:]
    y_sample = rmsnorm(hs, final_norm_g)
    return (y_prompt, y_sample, jnp.stack(pk), jnp.stack(pv), jnp.stack(pki), jnp.stack(ps), jnp.stack(pc), jnp.stack(sk), jnp.stack(sv), jnp.stack(ski), jnp.stack(ss), jnp.stack(sc))
```

```python
import functools

import numpy as np
import jax
import jax.numpy as jnp
from jax import lax
from jax.experimental import pallas as pl
from jax.experimental.pallas import tpu as pltpu

F32 = jnp.float32
BF16 = jnp.bfloat16
I32 = jnp.int32

D_MODEL = 1024
N_META = 16
EPS = 1e-6
MASK_NEG = -1e30
A_WIDTH = 512
A_HEAD = 128
A_HEADS = 4
B_WIDTH = 512
CONV_W = 31
C_HEADS = 8
C_HEAD_DIM = 64
C_WIDTH = 512
C_KV_HEADS = 2
IDX_HEADS = 4
IDX_DIM = 64
TOPK_MAX = 256
PAGE_SIZE = 128
ROPE_THETA = 500000.0
ROT_DIM = 16
IDX_SCALE = (IDX_HEADS * IDX_DIM) ** -0.5
INT_MIN = -(2 ** 31)

LANES = 128
SUBLANES = 8
VMEM_LIMIT = 48 * 1024 * 1024

N_IN = 8260
U_W = 8448
TN_IN = 768
BLK_AQ, BLK_AF, BLK_AI, BLK_AZ, BLK_GLUA, BLK_GLUG, BLK_BZ, BLK_CQ, BLK_KVQI, BLK_CZ = range(10)
BLK_GATE = 5
BLK_KIW = 64


def _cparams(sem):
    return pltpu.CompilerParams(dimension_semantics=sem, vmem_limit_bytes=VMEM_LIMIT)


def _sigmoid(x):
    return 1.0 / (1.0 + jnp.exp(-x))


def _nt(a, b):
    return lax.dot_general(a, b, (((1,), (1,)), ((), ())), preferred_element_type=F32)


def _tn(a, b):
    return lax.dot_general(a, b, (((0,), (0,)), ((), ())), preferred_element_type=F32)


def _sort_key(x):
    bits = pltpu.bitcast(x, I32)
    return bits ^ ((bits >> 31) & jnp.int32(0x7FFFFFFF))


def _inproj_kernel(x_ref, g_ref, w_ref, b_ref, o_ref, xn_ref):
    @pl.when(pl.program_id(1) == 0)
    def _():
        x = x_ref[...]
        ms = jnp.mean(x * x, axis=-1, keepdims=True)
        xn_ref[...] = (x * lax.rsqrt(ms + EPS) * g_ref[...]).astype(BF16)

    o_ref[...] = jnp.dot(xn_ref[...], w_ref[...], preferred_element_type=F32) + b_ref[...]


def _inproj(h, g, w, b):
    rows = h.shape[0]
    tm = next(t for t in (1024, 512, 256, 128, 64, 32, 16, 8) if rows % t == 0)
    return pl.pallas_call(
        _inproj_kernel,
        grid=(rows // tm, U_W // TN_IN),
        in_specs=[
            pl.BlockSpec((tm, D_MODEL), lambda i, j: (i, 0)),
            pl.BlockSpec((1, D_MODEL), lambda i, j: (0, 0)),
            pl.BlockSpec((D_MODEL, TN_IN), lambda i, j: (0, j)),
            pl.BlockSpec((1, TN_IN), lambda i, j: (0, j)),
        ],
        out_specs=pl.BlockSpec((tm, TN_IN), lambda i, j: (i, j)),
        out_shape=jax.ShapeDtypeStruct((rows, U_W), F32),
        scratch_shapes=[pltpu.VMEM((tm, D_MODEL), BF16)],
        compiler_params=_cparams(("parallel", "arbitrary")),
        name="inproj",
    )(h, g, w, b)


def _rope_kernel(cq_ref, kvqi_ref, kiw_ref, cos_ref, sin_ref, q_o, qi_o, kv_o, kiw_o):
    cosd = cos_ref[...]
    sind = sin_ref[...]
    lane = lax.broadcasted_iota(I32, (1, LANES), 1)
    l64 = lane % C_HEAD_DIM
    half = ROT_DIM // 2
    c2 = jnp.where(l64 < ROT_DIM, cosd, 1.0)
    a2 = jnp.where((l64 >= half) & (l64 < ROT_DIM), sind, 0.0)
    b2 = jnp.where(l64 < half, -sind, 0.0)
    c1 = jnp.where(lane < ROT_DIM, cosd, 1.0)
    a1 = jnp.where((lane >= half) & (lane < ROT_DIM), sind, 0.0)
    b1 = jnp.where(lane < half, -sind, 0.0)

    def rot(x, c, a, b):
        return x * c + pltpu.roll(x, half, 1) * a + pltpu.roll(x, LANES - half, 1) * b

    scale = C_HEAD_DIM ** -0.5
    for j in range(4):
        sl = slice(j * LANES, (j + 1) * LANES)
        q_o[:, sl] = rot(cq_ref[:, sl], c2, a2, b2) * scale
    kv_o[:, 0:LANES] = rot(kvqi_ref[:, 0:LANES], c2, a2, b2)
    kv_o[:, LANES:2 * LANES] = kvqi_ref[:, LANES:2 * LANES]
    for j in range(2):
        qi_o[:, j * LANES:(j + 1) * LANES] = rot(kvqi_ref[:, (2 + j) * LANES:(3 + j) * LANES], c2, a2, b2)
    kiw_o[...] = rot(kiw_ref[...], c1, a1, b1)


def _rope(u, cosd, sind):
    rows = u.shape[0]
    tm = next(t for t in (512, 256, 128, 64, 32, 16, 8) if rows % t == 0)
    return pl.pallas_call(
        _rope_kernel,
        grid=(rows // tm,),
        in_specs=[
            pl.BlockSpec((tm, 512), lambda i: (i, BLK_CQ)),
            pl.BlockSpec((tm, 512), lambda i: (i, BLK_KVQI)),
            pl.BlockSpec((tm, LANES), lambda i: (i, BLK_KIW)),
            pl.BlockSpec((tm, LANES), lambda i: (i, 0)),
            pl.BlockSpec((tm, LANES), lambda i: (i, 0)),
        ],
        out_specs=[
            pl.BlockSpec((tm, 512), lambda i: (i, 0)),
            pl.BlockSpec((tm, 256), lambda i: (i, 0)),
            pl.BlockSpec((tm, 256), lambda i: (i, 0)),
            pl.BlockSpec((tm, LANES), lambda i: (i, 0)),
        ],
        out_shape=[
            jax.ShapeDtypeStruct((rows, 512), F32),
            jax.ShapeDtypeStruct((rows, 256), F32),
            jax.ShapeDtypeStruct((rows, 256), F32),
            jax.ShapeDtypeStruct((rows, LANES), F32),
        ],
        compiler_params=_cparams(("parallel",)),
        name="rope",
    )(u, u, u, cosd, sind)


def _hgrn_kernel(*refs, layer, sb, rb, t_valid, has_s0):
    aq_ref, af_ref, ai_ref, az_ref, lbl_ref, gn_ref = refs[:6]
    if has_s0:
        s0_ref, o_ref, sout_ref, st_ref = refs[6:]
    else:
        o_ref, sout_ref, st_ref = refs[6:]
    j = pl.program_id(1)
    nsub_max = rb // sb

    @pl.when(j == 0)
    def _():
        for h in range(A_HEADS):
            if has_s0:
                st_ref[h] = s0_ref[h].T
            else:
                st_ref[h] = jnp.zeros((A_HEAD, A_HEAD), F32)

    lg = lbl_ref[...]
    e = jnp.exp(lg - jnp.max(lg, axis=0, keepdims=True))
    sm = e / jnp.sum(e, axis=0, keepdims=True)
    lb = jnp.zeros((1, A_WIDTH), F32)
    for i in range(1, layer + 1):
        lb = lb + sm[i:i + 1, :]

    n_sub = jnp.clip((t_valid - j * rb) // sb, 0, nsub_max)
    if nsub_max > 1:
        o_ref[...] = jnp.zeros((rb, A_WIDTH), F32)
    ri = lax.broadcasted_iota(I32, (sb, sb), 0)
    ci = lax.broadcasted_iota(I32, (sb, sb), 1)
    tri = (ri >= ci).astype(F32)
    rowi = lax.broadcasted_iota(I32, (sb, 1), 0)
    gn = gn_ref[...]

    def sub(i, carry):
        r0 = pl.multiple_of(i * sb, sb)
        f = af_ref[pl.ds(r0, sb), :]
        aq = aq_ref[pl.ds(r0, sb), :]
        v = ai_ref[pl.ds(r0, sb), :]
        az = az_ref[pl.ds(r0, sb), :]
        logf = jnp.log(lb + (1.0 - lb) * _sigmoid(f))
        kk = (1.0 - lb) * _sigmoid(-f)
        q = aq * _sigmoid(aq)
        b = jnp.dot(tri, logf, preferred_element_type=F32, precision=lax.Precision.HIGHEST)
        blast = b[sb - 1:sb, :]
        qe = q * jnp.exp(b)
        kd = kk * jnp.exp(blast - b)
        dec = jnp.exp(blast)
        outs = []
        for h in range(A_HEADS):
            sl = slice(h * A_HEAD, (h + 1) * A_HEAD)
            st = st_ref[h]
            o = _nt(qe[:, sl].astype(BF16), st.astype(BF16))
            qh, kh, vh, bh = q[:, sl], kk[:, sl], v[:, sl], b[:, sl]
            for s in range(sb):
                w = jnp.exp(jnp.minimum(bh - bh[s:s + 1, :], 0.0))
                a = jnp.sum(qh * (kh[s:s + 1, :] * w), axis=-1, keepdims=True)
                a = jnp.where(rowi >= s, a, 0.0)
                o = o + a * vh[s:s + 1, :]
            st_ref[h] = st * dec[:, sl] + _tn(vh.astype(BF16), kd[:, sl].astype(BF16))
            on = o * lax.rsqrt(jnp.mean(o * o, axis=-1, keepdims=True) + EPS) * gn
            outs.append(on)
        o_ref[pl.ds(r0, sb), :] = jnp.concatenate(outs, axis=1) * (az * _sigmoid(az))
        return carry

    lax.fori_loop(0, n_sub, sub, 0)

    @pl.when(j == pl.num_programs(1) - 1)
    def _():
        for h in range(A_HEADS):
            sout_ref[h] = st_ref[h].T


def _hgrn(u, lb_logits, gn_g, s0, *, layer, nb, t_pad, t_valid, rb, sb, state_layer=None):
    nj = t_pad // rb
    has_s0 = s0 is not None

    def ublk(blk):
        return pl.BlockSpec((rb, 512), lambda b, j: (b * nj + j, blk))

    in_specs = [ublk(BLK_AQ), ublk(BLK_AF), ublk(BLK_AI), ublk(BLK_AZ),
                pl.BlockSpec(lb_logits.shape, lambda b, j: (0, 0)),
                pl.BlockSpec((1, A_HEAD), lambda b, j: (0, 0))]
    args = [u, u, u, u, lb_logits, gn_g]
    if has_s0:
        in_specs.append(pl.BlockSpec((None, None, A_HEADS, A_HEAD, A_HEAD), lambda b, j: (state_layer, b, 0, 0, 0)))
        args.append(s0)
    return pl.pallas_call(
        functools.partial(_hgrn_kernel, layer=layer, sb=sb, rb=rb, t_valid=t_valid, has_s0=has_s0),
        grid=(nb, nj),
        in_specs=in_specs,
        out_specs=[pl.BlockSpec((rb, A_WIDTH), lambda b, j: (b * nj + j, 0)),
                   pl.BlockSpec((None, A_HEADS, A_HEAD, A_HEAD), lambda b, j: (b, 0, 0, 0))],
        out_shape=[jax.ShapeDtypeStruct((nb * t_pad, A_WIDTH), F32),
                   jax.ShapeDtypeStruct((nb, A_HEADS, A_HEAD, A_HEAD), F32)],
        scratch_shapes=[pltpu.VMEM((A_HEADS, A_HEAD, A_HEAD), F32)],
        compiler_params=_cparams(("parallel", "arbitrary")),
        name="hgrn",
    )(*args)


def _conv_tail(y, bz, lng, lnb, pw_ref):
    mu = jnp.mean(y, axis=-1, keepdims=True)
    d = y - mu
    var = jnp.mean(d * d, axis=-1, keepdims=True)
    y = d * lax.rsqrt(var + EPS) * lng + lnb
    y = y * _sigmoid(y)
    z = jnp.dot(y.astype(BF16), pw_ref[...], preferred_element_type=F32)
    return z * (bz * _sigmoid(bz))


def _conv_prompt_kernel(a_ref, g_ref, bz_ref, w_ref, cb_ref, lng_ref, lnb_ref, pw_ref, o_ref, st_ref, xb_ref, *, rb, tail_off):
    j = pl.program_id(1)
    halo = 32

    @pl.when(j == 0)
    def _():
        xb_ref[0:halo, :] = jnp.zeros((halo, B_WIDTH), F32)

    xb_ref[halo:halo + rb, :] = a_ref[...] * _sigmoid(g_ref[...])
    acc = jnp.zeros((rb, B_WIDTH), F32) + cb_ref[...]
    for t in range(CONV_W):
        acc = acc + xb_ref[pl.ds(halo - (CONV_W - 1) + t, rb), :] * w_ref[t:t + 1, :]
    o_ref[...] = _conv_tail(acc, bz_ref[...], lng_ref[...], lnb_ref[...], pw_ref)

    @pl.when(j == pl.num_programs(1) - 1)
    def _():
        st_ref[...] = xb_ref[pl.ds(tail_off, CONV_W - 1), :]

    xb_ref[0:halo, :] = xb_ref[rb:rb + halo, :]


def _conv_prompt(u, cw, cb, lng, lnb, pw, *, nb, t_pad, t_valid):
    rb = 128
    nj = t_pad // rb
    tail_off = (t_valid - (CONV_W - 1)) - ((nj - 1) * rb - 32)
    assert 0 <= tail_off and tail_off + CONV_W - 1 <= rb + 32

    def ublk(blk):
        return pl.BlockSpec((rb, 512), lambda b, j: (b * nj + j, blk))

    def full(a):
        return pl.BlockSpec(a.shape, lambda b, j: (0,) * a.ndim)

    return pl.pallas_call(
        functools.partial(_conv_prompt_kernel, rb=rb, tail_off=tail_off),
        grid=(nb, nj),
        in_specs=[ublk(BLK_GLUA), ublk(BLK_GLUG), ublk(BLK_BZ), full(cw), full(cb), full(lng), full(lnb), full(pw)],
        out_specs=[pl.BlockSpec((rb, B_WIDTH), lambda b, j: (b * nj + j, 0)),
                   pl.BlockSpec((None, CONV_W - 1, B_WIDTH), lambda b, j: (b, 0, 0))],
        out_shape=[jax.ShapeDtypeStruct((nb * t_pad, B_WIDTH), F32),
                   jax.ShapeDtypeStruct((nb, CONV_W - 1, B_WIDTH), F32)],
        scratch_shapes=[pltpu.VMEM((rb + 32, B_WIDTH), F32)],
        compiler_params=_cparams(("parallel", "arbitrary")),
        name="conv_prompt",
    )(u, u, u, cw, cb, lng, lnb, pw)


def _conv_decode_kernel(a_ref, g_ref, bz_ref, sin_ref, w_ref, cb_ref, lng_ref, lnb_ref, pw_ref, o_ref, sout_ref, xs_ref, *, ns, ds):
    nbuf = CONV_W - 1
    xs_ref[:, 0:nbuf, :] = sin_ref[...]
    glu = a_ref[...] * _sigmoid(g_ref[...])
    xs_ref[:, nbuf:nbuf + ds, :] = glu.reshape(ns, ds, B_WIDTH)
    acc = jnp.zeros((ns, ds, B_WIDTH), F32) + cb_ref[...]
    for t in range(CONV_W):
        acc = acc + xs_ref[:, pl.ds(t, ds), :] * w_ref[t:t + 1, :]
    o_ref[...] = _conv_tail(acc.reshape(ns * ds, B_WIDTH), bz_ref[...], lng_ref[...], lnb_ref[...], pw_ref)
    sout_ref[...] = xs_ref[:, pl.ds(ds, nbuf), :]


def _conv_decode(u, state, cw, cb, lng, lnb, pw, *, layer, nb, ds):
    ns = next(t for t in (16, 8, 4, 2, 1) if nb % t == 0)
    rb = ns * ds

    def ublk(blk):
        return pl.BlockSpec((rb, 512), lambda i: (i, blk))

    def full(a):
        return pl.BlockSpec(a.shape, lambda i: (0,) * a.ndim)

    return pl.pallas_call(
        functools.partial(_conv_decode_kernel, ns=ns, ds=ds),
        grid=(nb // ns,),
        in_specs=[ublk(BLK_GLUA), ublk(BLK_GLUG), ublk(BLK_BZ),
                  pl.BlockSpec((None, ns, CONV_W - 1, B_WIDTH), lambda i: (layer, i, 0, 0)),
                  full(cw), full(cb), full(lng), full(lnb), full(pw)],
        out_specs=[pl.BlockSpec((rb, B_WIDTH), lambda i: (i, 0)),
                   pl.BlockSpec((ns, CONV_W - 1, B_WIDTH), lambda i: (i, 0, 0))],
        out_shape=[jax.ShapeDtypeStruct((nb * ds, B_WIDTH), F32),
                   jax.ShapeDtypeStruct((nb, CONV_W - 1, B_WIDTH), F32)],
        scratch_shapes=[pltpu.VMEM((ns, CONV_W - 1 + ds, B_WIDTH), F32)],
        compiler_params=_cparams(("parallel",)),
        name="conv_decode",
    )(u, u, u, state, cw, cb, lng, lnb, pw)


def _dsa_prompt_kernel(q_ref, qi_ref, cz_ref, kv_ref, kiw_ref, o_ref, k16, vt16, ki16, ik, bias, *, n_sel, nblk):
    i = pl.program_id(1)
    blk = LANES
    hd = C_HEAD_DIM

    @pl.when(i == 0)
    def _():
        for kb in range(nblk):
            sl = slice(kb * blk, (kb + 1) * blk)
            x = kv_ref[sl, :]
            for hk in range(C_KV_HEADS):
                k16[hk * nblk + kb] = x[:, hk * hd:(hk + 1) * hd].astype(BF16)
            vt16[kb] = x[:, blk:2 * blk].T.astype(BF16)
            ki16[kb] = kiw_ref[sl, :][:, 0:IDX_DIM].astype(BF16)

    nkb = i + 1
    r0 = pl.multiple_of(i * blk, blk)
    w_t = kiw_ref[pl.ds(r0, blk), :].T
    qi = qi_ref[...].astype(BF16)
    rowi = lax.broadcasted_iota(I32, (blk, blk), 0)
    lanei = lax.broadcasted_iota(I32, (blk, blk), 1)
    qpos = r0 + lanei

    def p1(kb, c):
        kib = ki16[kb]
        acc = jnp.zeros((blk, blk), F32)
        for h in range(IDX_HEADS):
            s = _nt(kib, qi[:, h * IDX_DIM:(h + 1) * IDX_DIM])
            acc = acc + jnp.maximum(s, 0.0) * (w_t[IDX_DIM + h:IDX_DIM + h + 1, :] * IDX_SCALE)
        allowed = (kb * blk + rowi) <= qpos
        ik[kb] = jnp.where(allowed, _sort_key(acc), jnp.int32(INT_MIN))
        return c

    lax.fori_loop(0, nkb, p1, 0)

    def count_ge(cand):
        def body(kb, c):
            m = jnp.where(ik[kb] >= cand, 1, 0)
            return c + jnp.sum(m.reshape(blk // SUBLANES, SUBLANES, blk), axis=0)
        c = lax.fori_loop(0, nkb, body, jnp.zeros((SUBLANES, blk), I32))
        return jnp.sum(c, axis=0, keepdims=True)

    zero = jnp.zeros((1, blk), I32)
    thr = jnp.where(count_ge(zero) >= n_sel, zero, jnp.int32(INT_MIN))

    def bit_body(t, thr):
        cand = thr + lax.shift_left(jnp.int32(1), jnp.int32(30) - t)
        return jnp.where(count_ge(cand) >= n_sel, cand, thr)

    thr = lax.fori_loop(0, 31, bit_body, thr)

    def gt_body(kb, c):
        m = jnp.where(ik[kb] > thr, 1, 0)
        return c + jnp.sum(m.reshape(blk // SUBLANES, SUBLANES, blk), axis=0)

    c_gt = jnp.sum(lax.fori_loop(0, nkb, gt_body, jnp.zeros((SUBLANES, blk), I32)), axis=0, keepdims=True)
    need = (n_sel - c_gt).astype(F32)
    ltri = (rowi >= lanei).astype(BF16)

    def p3(kb, carry):
        key = ik[kb]
        tie = key == thr
        pre = jnp.dot(ltri, jnp.where(tie, 1.0, 0.0).astype(BF16), preferred_element_type=F32) + carry
        allowed = (kb * blk + rowi) <= qpos
        sel = ((tie & (pre <= need)) | (key > thr)) & allowed
        bias[kb] = jnp.where(sel, 0.0, MASK_NEG)
        return pre[blk - 1:blk, :]

    lax.fori_loop(0, nkb, p3, jnp.zeros((1, blk), F32))

    q = q_ref[...].astype(BF16)
    outs = []
    for h in range(C_HEADS):
        qh = q[:, h * hd:(h + 1) * hd]
        hk = h // (C_HEADS // C_KV_HEADS)

        def body(kb, carry, qh=qh, hk=hk):
            m, l, acc = carry
            s = _nt(k16[hk * nblk + kb], qh) + bias[kb]
            mn = jnp.maximum(m, jnp.max(s, axis=0, keepdims=True))
            a = jnp.exp(m - mn)
            p = jnp.exp(s - mn)
            l = a * l + jnp.sum(p, axis=0, keepdims=True)
            vt = vt16[kb][hk * hd:(hk + 1) * hd, :]
            acc = a * acc + jnp.dot(vt, p.astype(BF16), preferred_element_type=F32)
            return mn, l, acc

        init = (jnp.full((1, blk), -3.0e38, F32), jnp.zeros((1, blk), F32), jnp.zeros((hd, blk), F32))
        m, l, acc = lax.fori_loop(0, nkb, body, init)
        outs.append(acc / l)
    o_t = jnp.concatenate(outs, axis=0)
    o = jnp.concatenate([o_t[j * blk:(j + 1) * blk, :].T for j in range(C_WIDTH // blk)], axis=1)
    cz = cz_ref[...]
    o_ref[...] = o * (cz * _sigmoid(cz))


def _dsa_prompt(u, qr, qir, kv, kiw, *, nb, t_pad, n_sel):
    blk = LANES
    nblk = t_pad // blk
    return pl.pallas_call(
        functools.partial(_dsa_prompt_kernel, n_sel=n_sel, nblk=nblk),
        grid=(nb, nblk),
        in_specs=[
            pl.BlockSpec((blk, 512), lambda b, i: (b * nblk + i, 0)),
            pl.BlockSpec((blk, 256), lambda b, i: (b * nblk + i, 0)),
            pl.BlockSpec((blk, 512), lambda b, i: (b * nblk + i, BLK_CZ)),
            pl.BlockSpec((t_pad, 256), lambda b, i: (b, 0)),
            pl.BlockSpec((t_pad, LANES), lambda b, i: (b, 0)),
        ],
        out_specs=pl.BlockSpec((blk, C_WIDTH), lambda b, i: (b * nblk + i, 0)),
        out_shape=jax.ShapeDtypeStruct((nb * t_pad, C_WIDTH), F32),
        scratch_shapes=[
            pltpu.VMEM((C_KV_HEADS * nblk, blk, C_HEAD_DIM), BF16),
            pltpu.VMEM((nblk, blk, blk), BF16),
            pltpu.VMEM((nblk, blk, IDX_DIM), BF16),
            pltpu.VMEM((nblk, blk, blk), I32),
            pltpu.VMEM((nblk, blk, blk), F32),
        ],
        compiler_params=_cparams(("parallel", "arbitrary")),
        name="dsa_prompt",
    )(qr, qir, u, kv, kiw)


def _dsa_decode_kernel(pt_ref, q_ref, qi_ref, cz_ref, kvn_ref, kiwn_ref, *rest, n_sel, npages, ds):
    kpages = rest[0:npages]
    vpages = rest[npages:2 * npages]
    ipages = rest[2 * npages:3 * npages]
    o_ref = rest[3 * npages]
    ikey_ref = rest[3 * npages + 1]
    del pt_ref
    blk = LANES
    hd = C_HEAD_DIM
    nblk = npages + 1
    groups = C_HEADS // C_KV_HEADS

    kiwn = kiwn_ref[...]
    kvn = kvn_ref[...]
    qi = qi_ref[...]
    qis = jnp.concatenate([qi[:, h * IDX_DIM:(h + 1) * IDX_DIM] for h in range(IDX_HEADS)], axis=0).astype(BF16)
    wcol = jnp.concatenate([kiwn[:, IDX_DIM + h:IDX_DIM + h + 1] for h in range(IDX_HEADS)], axis=0) * IDX_SCALE
    zpad = jnp.zeros((blk - ds, blk), F32)
    knew = jnp.concatenate([kvn[:, 0:blk], zpad], axis=0)
    vnew = jnp.concatenate([kvn[:, blk:2 * blk], zpad], axis=0)
    inew = jnp.concatenate([kiwn, zpad], axis=0)[:, 0:IDX_DIM]
    rowq = lax.broadcasted_iota(I32, (ds, blk), 0)
    lane = lax.broadcasted_iota(I32, (ds, blk), 1)
    allowed_new = lane <= rowq

    def iscore(kpage):
        r = jnp.maximum(_nt(qis, kpage.astype(BF16)), 0.0) * wcol
        acc = r[0:ds]
        for h in range(1, IDX_HEADS):
            acc = acc + r[h * ds:(h + 1) * ds]
        return _sort_key(jnp.zeros((ds, blk), F32) + acc)

    for j in range(npages):
        ikey_ref[:, j * blk:(j + 1) * blk] = iscore(ipages[j][...])
    ikey_ref[:, npages * blk:nblk * blk] = jnp.where(allowed_new, iscore(inew), jnp.int32(INT_MIN))
    keys = ikey_ref[...]

    def count_ge(cand):
        return jnp.sum(jnp.where(keys >= cand, 1, 0), axis=1, keepdims=True)

    thr = jnp.full((ds, 1), INT_MIN, I32)
    for step in range(8):
        shift = 28 - 4 * step
        digit = jnp.zeros((ds, 1), I32)
        for c in range(1, 16):
            inc = int(np.array(c << shift, dtype=np.uint32).astype(np.int32))
            digit = digit + jnp.where(count_ge(thr + jnp.int32(inc)) >= n_sel, 1, 0)
        thr = thr + digit * jnp.int32(1 << shift)

    gt = keys > thr
    tie = keys == thr
    need = (n_sel - jnp.sum(jnp.where(gt, 1, 0), axis=1, keepdims=True)).astype(F32)
    tief = jnp.where(tie, 1.0, 0.0)
    ri = lax.broadcasted_iota(I32, (blk, blk), 0)
    ci = lax.broadcasted_iota(I32, (blk, blk), 1)
    utri = (ri <= ci).astype(F32)
    carry = jnp.zeros((ds, 1), F32)
    biases = []
    for j in range(nblk):
        sl = slice(j * blk, (j + 1) * blk)
        pre = jnp.dot(tief[:, sl], utri, preferred_element_type=F32) + carry
        sel = (tie[:, sl] & (pre <= need)) | gt[:, sl]
        if j == npages:
            sel = sel & allowed_new
        biases.append(jnp.where(sel, 0.0, MASK_NEG))
        carry = pre[:, blk - 1:blk]
    bias = jnp.concatenate(biases, axis=1)
    bias_g = jnp.concatenate([bias] * groups, axis=0)

    q = q_ref[...]
    outs = [None] * C_HEADS
    for hk in range(C_KV_HEADS):
        qg = jnp.concatenate([q[:, (hk * groups + g) * hd:(hk * groups + g + 1) * hd] for g in range(groups)], axis=0).astype(BF16)
        ksl = slice(hk * hd, (hk + 1) * hd)
        ls = [_nt(qg, kpages[j][...][:, ksl].astype(BF16)) for j in range(npages)]
        ls.append(_nt(qg, knew[:, ksl].astype(BF16)))
        s = jnp.concatenate(ls, axis=1) + bias_g
        m = jnp.max(s, axis=1, keepdims=True)
        p = jnp.exp(s - m)
        l = jnp.sum(p, axis=1, keepdims=True)
        pb = p.astype(BF16)
        acc = jnp.dot(pb[:, npages * blk:nblk * blk], vnew[:, ksl].astype(BF16), preferred_element_type=F32)
        for j in range(npages):
            acc = acc + jnp.dot(pb[:, j * blk:(j + 1) * blk], vpages[j][...][:, ksl].astype(BF16), preferred_element_type=F32)
        og = acc / l
        for g in range(groups):
            outs[hk * groups + g] = og[g * ds:(g + 1) * ds, :]
    o = jnp.concatenate(outs, axis=1)
    cz = cz_ref[...]
    o_ref[...] = o * (cz * _sigmoid(cz))


def _dsa_decode(u, qr, qir, kv, kiw, cache_k, cache_v, cache_kidx, page_table, *, layer, nb, ds, n_sel):
    npages = page_table.shape[1]
    pt = page_table.reshape(-1)

    def page(j, width):
        return pl.BlockSpec((None, None, PAGE_SIZE, width), lambda b, pt_ref: (layer, pt_ref[b * npages + j], 0, 0))

    in_specs = [
        pl.BlockSpec((ds, 512), lambda b, pt_ref: (b, 0)),
        pl.BlockSpec((ds, 256), lambda b, pt_ref: (b, 0)),
        pl.BlockSpec((ds, 512), lambda b, pt_ref: (b, BLK_CZ)),
        pl.BlockSpec((ds, 256), lambda b, pt_ref: (b, 0)),
        pl.BlockSpec((ds, LANES), lambda b, pt_ref: (b, 0)),
    ]
    in_specs += [page(j, LANES) for j in range(npages)]
    in_specs += [page(j, LANES) for j in range(npages)]
    in_specs += [page(j, IDX_DIM) for j in range(npages)]
    grid_spec = pltpu.PrefetchScalarGridSpec(
        num_scalar_prefetch=1,
        grid=(nb,),
        in_specs=in_specs,
        out_specs=pl.BlockSpec((ds, C_WIDTH), lambda b, pt_ref: (b, 0)),
        scratch_shapes=[pltpu.VMEM((ds, (npages + 1) * LANES), I32)],
    )
    args = [pt, qr, qir, u, kv, kiw] + [cache_k] * npages + [cache_v] * npages + [cache_kidx] * npages
    return pl.pallas_call(
        functools.partial(_dsa_decode_kernel, n_sel=n_sel, npages=npages, ds=ds),
        grid_spec=grid_spec,
        out_shape=jax.ShapeDtypeStruct((nb * ds, C_WIDTH), F32),
        compiler_params=_cparams(("arbitrary",)),
        name="dsa_decode",
    )(*args)


def _merge_kernel(h_ref, ya_ref, yb_ref, yc_ref, ga_ref, gb_ref, gc_ref, wpa_ref, wpb_ref, wpc_ref, wo_ref, fg_ref, o_ref, *, final):
    def proj(y_ref, w_ref):
        return jnp.dot(y_ref[...].astype(BF16), w_ref[...], preferred_element_type=F32)

    m = (_sigmoid(ga_ref[...]) * proj(ya_ref, wpa_ref)
         + _sigmoid(gb_ref[...]) * proj(yb_ref, wpb_ref)
         + _sigmoid(gc_ref[...]) * proj(yc_ref, wpc_ref))
    h = h_ref[...] + jnp.dot(m.astype(BF16), wo_ref[...], preferred_element_type=F32)
    if final:
        h = h * lax.rsqrt(jnp.mean(h * h, axis=-1, keepdims=True) + EPS) * fg_ref[...]
    o_ref[...] = h


def _merge(h, ya, yb, yc, u, wpa, wpb, wpc, wo, fg, *, final):
    rows = h.shape[0]
    tm = next(t for t in (512, 256, 128, 64, 32, 16, 8) if rows % t == 0)

    def rowblk(width, blk=0):
        return pl.BlockSpec((tm, width), lambda i: (i, blk))

    def full(a):
        return pl.BlockSpec(a.shape, lambda i: (0,) * a.ndim)

    return pl.pallas_call(
        functools.partial(_merge_kernel, final=final),
        grid=(rows // tm,),
        in_specs=[rowblk(D_MODEL), rowblk(512), rowblk(512), rowblk(512),
                  rowblk(D_MODEL, BLK_GATE), rowblk(D_MODEL, BLK_GATE + 1), rowblk(D_MODEL, BLK_GATE + 2),
                  full(wpa), full(wpb), full(wpc), full(wo), full(fg)],
        out_specs=rowblk(D_MODEL),
        out_shape=jax.ShapeDtypeStruct((rows, D_MODEL), F32),
        compiler_params=_cparams(("parallel",)),
        name="merge",
    )(h, ya, yb, yc, u, u, u, wpa, wpb, wpc, wo, fg)


def _rope_tables(pos):
    half = ROT_DIM // 2
    inv = ROPE_THETA ** (-jnp.arange(half, dtype=F32) * 2.0 / ROT_DIM)
    ang = pos.astype(F32)[:, None] * jnp.tile(inv, LANES // half)[None, :]
    return jnp.cos(ang), jnp.sin(ang)


def _pack_cols(w):
    split = 4608
    tail = split + 68
    pad = jnp.zeros(w.shape[:-1] + (U_W - N_IN,), w.dtype)
    return jnp.concatenate([w[..., :split], w[..., tail:], w[..., split:tail], pad], axis=-1)


def kernel(x_prompt, x_sample, cache_k, cache_v, cache_kidx, state_hgrn, state_conv, page_table, meta_tokens, norm_g, w_in, b_in, lb_logits, hgrn_norm_g, conv_w, conv_b, conv_ln_g, conv_ln_b, conv_pw, w_pa, w_pb, w_pc, w_out, final_norm_g):
    nbp, seq, _ = x_prompt.shape
    nbs, ds, _ = x_sample.shape
    depth = w_in.shape[0]
    npages = page_table.shape[1]
    past = npages * PAGE_SIZE
    t_valid = seq + N_META
    t_pad = -(-t_valid // LANES) * LANES
    n_sel_p = min(TOPK_MAX, t_valid // 4)
    n_sel_s = min(TOPK_MAX, (past + ds) // 4)
    n_phys = cache_k.shape[1]

    meta = jnp.broadcast_to(meta_tokens[None].astype(F32), (nbp, N_META, D_MODEL))
    hp = jnp.concatenate([meta, x_prompt, jnp.zeros((nbp, t_pad - t_valid, D_MODEL), F32)], axis=1).reshape(nbp * t_pad, D_MODEL)
    hs = x_sample.reshape(nbs * ds, D_MODEL)
    cos_p, sin_p = _rope_tables(jnp.tile(jnp.arange(t_pad), nbp))
    cos_s, sin_s = _rope_tables(jnp.tile(past + jnp.arange(ds), nbs))

    w_in_p = _pack_cols(w_in).astype(BF16)
    b_in_p = _pack_cols(b_in).reshape(depth, 1, U_W)
    ck4 = cache_k.reshape(depth, n_phys, PAGE_SIZE, C_KV_HEADS * C_HEAD_DIM)
    cv4 = cache_v.reshape(depth, n_phys, PAGE_SIZE, C_KV_HEADS * C_HEAD_DIM)
    fg = final_norm_g.reshape(1, D_MODEL)

    outs = {k: [] for k in ("pk", "pv", "pki", "ps", "pc", "sk", "sv", "ski", "ss", "sc")}
    for l in range(depth):
        g = norm_g[l].reshape(1, D_MODEL)
        gn = hgrn_norm_g[l].reshape(1, A_HEAD)
        cw, cb = conv_w[l], conv_b[l].reshape(1, B_WIDTH)
        lng, lnb = conv_ln_g[l].reshape(1, B_WIDTH), conv_ln_b[l].reshape(1, B_WIDTH)
        pw = conv_pw[l].astype(BF16)
        wpa, wpb, wpc, wo = (w[l].astype(BF16) for w in (w_pa, w_pb, w_pc, w_out))
        final = l == depth - 1

        u = _inproj(hp, g, w_in_p[l], b_in_p[l])
        qr, qir, kv, kiw = _rope(u, cos_p, sin_p)
        ya, s_new = _hgrn(u, lb_logits, gn, None, layer=l, nb=nbp, t_pad=t_pad, t_valid=t_valid, rb=128, sb=16)
        yb, c_new = _conv_prompt(u, cw, cb, lng, lnb, pw, nb=nbp, t_pad=t_pad, t_valid=t_valid)
        yc = _dsa_prompt(u, qr, qir, kv, kiw, nb=nbp, t_pad=t_pad, n_sel=n_sel_p)
        hp = _merge(hp, ya, yb, yc, u, wpa, wpb, wpc, wo, fg, final=final)
        kv3 = kv.reshape(nbp, t_pad, 2 * LANES)[:, :t_valid]
        outs["pk"].append(kv3[..., :LANES].reshape(nbp, t_valid, C_KV_HEADS, C_HEAD_DIM))
        outs["pv"].append(kv3[..., LANES:].reshape(nbp, t_valid, C_KV_HEADS, C_HEAD_DIM))
        outs["pki"].append(kiw.reshape(nbp, t_pad, LANES)[:, :t_valid, :IDX_DIM])
        outs["ps"].append(s_new)
        outs["pc"].append(c_new)

        u = _inproj(hs, g, w_in_p[l], b_in_p[l])
        qr, qir, kv, kiw = _rope(u, cos_s, sin_s)
        ya, s_new = _hgrn(u, lb_logits, gn, state_hgrn, layer=l, nb=nbs, t_pad=ds, t_valid=ds, rb=ds, sb=ds, state_layer=l)
        yb, c_new = _conv_decode(u, state_conv, cw, cb, lng, lnb, pw, layer=l, nb=nbs, ds=ds)
        yc = _dsa_decode(u, qr, qir, kv, kiw, ck4, cv4, cache_kidx, page_table, layer=l, nb=nbs, ds=ds, n_sel=n_sel_s)
        hs = _merge(hs, ya, yb, yc, u, wpa, wpb, wpc, wo, fg, final=final)
        kv3 = kv.reshape(nbs, ds, 2 * LANES)
        outs["sk"].append(kv3[..., :LANES].reshape(nbs, ds, C_KV_HEADS, C_HEAD_DIM))
        outs["sv"].append(kv3[..., LANES:].reshape(nbs, ds, C_KV_HEADS, C_HEAD_DIM))
        outs["ski"].append(kiw.reshape(nbs, ds, LANES)[..., :IDX_DIM])
        outs["ss"].append(s_new)
        outs["sc"].append(c_new)

    y_prompt = hp.reshape(nbp, t_pad, D_MODEL)[:, N_META:t_valid]
    y_sample = hs.reshape(nbs, ds, D_MODEL)
    st = {k: jnp.stack(v) for k, v in outs.items()}
    return (y_prompt, y_sample, st["pk"], st["pv"], st["pki"], st["ps"], st["pc"],
            st["sk"], st["sv"], st["ski"], st["ss"], st["sc"])
```

```python
import functools

import numpy as np
import jax
import jax.numpy as jnp
from jax import lax
from jax.experimental import pallas as pl
from jax.experimental.pallas import tpu as pltpu

F32 = jnp.float32
BF16 = jnp.bfloat16
I32 = jnp.int32
I16 = jnp.int16

D_MODEL = 1024
N_META = 16
EPS = 1e-6
MASK_NEG = -1e30
A_WIDTH = 512
A_HEAD = 128
A_HEADS = 4
B_WIDTH = 512
CONV_W = 31
C_HEADS = 8
C_HEAD_DIM = 64
C_WIDTH = 512
C_KV_HEADS = 2
IDX_HEADS = 4
IDX_DIM = 64
TOPK_MAX = 256
PAGE_SIZE = 128
ROPE_THETA = 500000.0
ROT_DIM = 16
IDX_SCALE = (IDX_HEADS * IDX_DIM) ** -0.5
INT_MIN = -(2 ** 31)

LANES = 128
SUBLANES = 8
VMEM_LIMIT = 48 * 1024 * 1024

N_IN = 8260
U_W = 8448
TN_IN = 768
BLK_AQ, BLK_AF, BLK_AI, BLK_AZ, BLK_GLUA, BLK_GLUG, BLK_BZ, BLK_CQ, BLK_KVQI, BLK_CZ = range(10)
BLK_GATE = 5
BLK_KIW = 64


def _cparams(sem):
    return pltpu.CompilerParams(dimension_semantics=sem, vmem_limit_bytes=VMEM_LIMIT)


def _sigmoid(x):
    return 1.0 / (1.0 + jnp.exp(-x))


def _nt(a, b):
    return lax.dot_general(a, b, (((1,), (1,)), ((), ())), preferred_element_type=F32)


def _tn(a, b):
    return lax.dot_general(a, b, (((0,), (0,)), ((), ())), preferred_element_type=F32)


def _sort_key(x):
    bits = pltpu.bitcast(x, I32)
    return bits ^ ((bits >> 31) & jnp.int32(0x7FFFFFFF))


def _inproj_kernel(x_ref, g_ref, w_ref, b_ref, o_ref, xn_ref):
    @pl.when(pl.program_id(1) == 0)
    def _():
        x = x_ref[...]
        ms = jnp.mean(x * x, axis=-1, keepdims=True)
        xn_ref[...] = (x * lax.rsqrt(ms + EPS) * g_ref[...]).astype(BF16)

    o_ref[...] = jnp.dot(xn_ref[...], w_ref[...], preferred_element_type=F32) + b_ref[...]


def _inproj(h, g, w, b):
    rows = h.shape[0]
    tm = next(t for t in (1024, 512, 256, 128, 64, 32, 16, 8) if rows % t == 0)
    return pl.pallas_call(
        _inproj_kernel,
        grid=(rows // tm, U_W // TN_IN),
        in_specs=[
            pl.BlockSpec((tm, D_MODEL), lambda i, j: (i, 0)),
            pl.BlockSpec((1, D_MODEL), lambda i, j: (0, 0)),
            pl.BlockSpec((D_MODEL, TN_IN), lambda i, j: (0, j)),
            pl.BlockSpec((1, TN_IN), lambda i, j: (0, j)),
        ],
        out_specs=pl.BlockSpec((tm, TN_IN), lambda i, j: (i, j)),
        out_shape=jax.ShapeDtypeStruct((rows, U_W), F32),
        scratch_shapes=[pltpu.VMEM((tm, D_MODEL), BF16)],
        compiler_params=_cparams(("parallel", "arbitrary")),
        name="inproj",
    )(h, g, w, b)


def _rope_kernel(cq_ref, kvqi_ref, kiw_ref, cos_ref, sin_ref, q_o, qi_o, kv_o, kiw_o):
    cosd = cos_ref[...]
    sind = sin_ref[...]
    lane = lax.broadcasted_iota(I32, (1, LANES), 1)
    l64 = lane % C_HEAD_DIM
    half = ROT_DIM // 2
    c2 = jnp.where(l64 < ROT_DIM, cosd, 1.0)
    a2 = jnp.where((l64 >= half) & (l64 < ROT_DIM), sind, 0.0)
    b2 = jnp.where(l64 < half, -sind, 0.0)
    c1 = jnp.where(lane < ROT_DIM, cosd, 1.0)
    a1 = jnp.where((lane >= half) & (lane < ROT_DIM), sind, 0.0)
    b1 = jnp.where(lane < half, -sind, 0.0)

    def rot(x, c, a, b):
        return x * c + pltpu.roll(x, half, 1) * a + pltpu.roll(x, LANES - half, 1) * b

    scale = C_HEAD_DIM ** -0.5
    for j in range(4):
        sl = slice(j * LANES, (j + 1) * LANES)
        q_o[:, sl] = rot(cq_ref[:, sl], c2, a2, b2) * scale
    kv_o[:, 0:LANES] = rot(kvqi_ref[:, 0:LANES], c2, a2, b2)
    kv_o[:, LANES:2 * LANES] = kvqi_ref[:, LANES:2 * LANES]
    for j in range(2):
        qi_o[:, j * LANES:(j + 1) * LANES] = rot(kvqi_ref[:, (2 + j) * LANES:(3 + j) * LANES], c2, a2, b2)
    kiw_o[...] = rot(kiw_ref[...], c1, a1, b1)


def _rope(u, cosd, sind):
    rows = u.shape[0]
    tm = next(t for t in (512, 256, 128, 64, 32, 16, 8) if rows % t == 0)
    return pl.pallas_call(
        _rope_kernel,
        grid=(rows // tm,),
        in_specs=[
            pl.BlockSpec((tm, 512), lambda i: (i, BLK_CQ)),
            pl.BlockSpec((tm, 512), lambda i: (i, BLK_KVQI)),
            pl.BlockSpec((tm, LANES), lambda i: (i, BLK_KIW)),
            pl.BlockSpec((tm, LANES), lambda i: (i, 0)),
            pl.BlockSpec((tm, LANES), lambda i: (i, 0)),
        ],
        out_specs=[
            pl.BlockSpec((tm, 512), lambda i: (i, 0)),
            pl.BlockSpec((tm, 256), lambda i: (i, 0)),
            pl.BlockSpec((tm, 256), lambda i: (i, 0)),
            pl.BlockSpec((tm, LANES), lambda i: (i, 0)),
        ],
        out_shape=[
            jax.ShapeDtypeStruct((rows, 512), F32),
            jax.ShapeDtypeStruct((rows, 256), F32),
            jax.ShapeDtypeStruct((rows, 256), F32),
            jax.ShapeDtypeStruct((rows, LANES), F32),
        ],
        compiler_params=_cparams(("parallel",)),
        name="rope",
    )(u, u, u, cosd, sind)


def _hgrn_kernel(*refs, layer, sb, rb, t_valid, has_s0):
    aq_ref, af_ref, ai_ref, az_ref, lbl_ref, gn_ref = refs[:6]
    if has_s0:
        s0_ref, o_ref, sout_ref, st_ref = refs[6:]
    else:
        o_ref, sout_ref, st_ref = refs[6:]
    j = pl.program_id(1)
    nsub_max = rb // sb

    @pl.when(j == 0)
    def _():
        for h in range(A_HEADS):
            if has_s0:
                st_ref[h] = s0_ref[h].T
            else:
                st_ref[h] = jnp.zeros((A_HEAD, A_HEAD), F32)

    lg = lbl_ref[...]
    e = jnp.exp(lg - jnp.max(lg, axis=0, keepdims=True))
    sm = e / jnp.sum(e, axis=0, keepdims=True)
    lb = jnp.zeros((1, A_WIDTH), F32)
    for i in range(1, layer + 1):
        lb = lb + sm[i:i + 1, :]

    n_sub = jnp.clip((t_valid - j * rb) // sb, 0, nsub_max)
    if nsub_max > 1:
        o_ref[...] = jnp.zeros((rb, A_WIDTH), F32)
    ri = lax.broadcasted_iota(I32, (sb, sb), 0)
    ci = lax.broadcasted_iota(I32, (sb, sb), 1)
    tri = (ri >= ci).astype(F32)
    rowi = lax.broadcasted_iota(I32, (sb, 1), 0)
    gn = gn_ref[...]

    def sub(i, carry):
        r0 = pl.multiple_of(i * sb, sb)
        f = af_ref[pl.ds(r0, sb), :]
        aq = aq_ref[pl.ds(r0, sb), :]
        v = ai_ref[pl.ds(r0, sb), :]
        az = az_ref[pl.ds(r0, sb), :]
        logf = jnp.log(lb + (1.0 - lb) * _sigmoid(f))
        kk = (1.0 - lb) * _sigmoid(-f)
        q = aq * _sigmoid(aq)
        b = jnp.dot(tri, logf, preferred_element_type=F32, precision=lax.Precision.HIGHEST)
        blast = b[sb - 1:sb, :]
        qe = q * jnp.exp(b)
        kd = kk * jnp.exp(blast - b)
        dec = jnp.exp(blast)
        outs = []
        for h in range(A_HEADS):
            sl = slice(h * A_HEAD, (h + 1) * A_HEAD)
            st = st_ref[h]
            o = _nt(qe[:, sl].astype(BF16), st.astype(BF16))
            qh, kh, vh, bh = q[:, sl], kk[:, sl], v[:, sl], b[:, sl]
            for s in range(sb):
                w = jnp.exp(bh - bh[s:s + 1, :])
                a = jnp.sum(qh * (kh[s:s + 1, :] * w), axis=-1, keepdims=True)
                o = o + jnp.where(rowi >= s, a, 0.0) * vh[s:s + 1, :]
            st_ref[h] = st * dec[:, sl] + _tn(vh.astype(BF16), kd[:, sl].astype(BF16))
            on = o * lax.rsqrt(jnp.mean(o * o, axis=-1, keepdims=True) + EPS) * gn
            outs.append(on)
        o_ref[pl.ds(r0, sb), :] = jnp.concatenate(outs, axis=1) * (az * _sigmoid(az))
        return carry

    lax.fori_loop(0, n_sub, sub, 0)

    @pl.when(j == pl.num_programs(1) - 1)
    def _():
        for h in range(A_HEADS):
            sout_ref[h] = st_ref[h].T


def _hgrn(u, lb_logits, gn_g, s0, *, layer, nb, t_pad, t_valid, rb, sb, state_layer=None):
    nj = t_pad // rb
    has_s0 = s0 is not None

    def ublk(blk):
        return pl.BlockSpec((rb, 512), lambda b, j: (b * nj + j, blk))

    in_specs = [ublk(BLK_AQ), ublk(BLK_AF), ublk(BLK_AI), ublk(BLK_AZ),
                pl.BlockSpec(lb_logits.shape, lambda b, j: (0, 0)),
                pl.BlockSpec((1, A_HEAD), lambda b, j: (0, 0))]
    args = [u, u, u, u, lb_logits, gn_g]
    if has_s0:
        in_specs.append(pl.BlockSpec((None, None, A_HEADS, A_HEAD, A_HEAD), lambda b, j: (state_layer, b, 0, 0, 0)))
        args.append(s0)
    return pl.pallas_call(
        functools.partial(_hgrn_kernel, layer=layer, sb=sb, rb=rb, t_valid=t_valid, has_s0=has_s0),
        grid=(nb, nj),
        in_specs=in_specs,
        out_specs=[pl.BlockSpec((rb, A_WIDTH), lambda b, j: (b * nj + j, 0)),
                   pl.BlockSpec((None, A_HEADS, A_HEAD, A_HEAD), lambda b, j: (b, 0, 0, 0))],
        out_shape=[jax.ShapeDtypeStruct((nb * t_pad, A_WIDTH), F32),
                   jax.ShapeDtypeStruct((nb, A_HEADS, A_HEAD, A_HEAD), F32)],
        scratch_shapes=[pltpu.VMEM((A_HEADS, A_HEAD, A_HEAD), F32)],
        compiler_params=_cparams(("parallel", "arbitrary")),
        name="hgrn",
    )(*args)


def _conv_tail(y, bz, lng, lnb, pw_ref):
    mu = jnp.mean(y, axis=-1, keepdims=True)
    d = y - mu
    var = jnp.mean(d * d, axis=-1, keepdims=True)
    y = d * lax.rsqrt(var + EPS) * lng + lnb
    y = y * _sigmoid(y)
    z = jnp.dot(y.astype(BF16), pw_ref[...], preferred_element_type=F32)
    return z * (bz * _sigmoid(bz))


def _conv_prompt_kernel(a_ref, g_ref, bz_ref, w_ref, cb_ref, lng_ref, lnb_ref, pw_ref, o_ref, st_ref, xb_ref, y_ref, xs_ref, *, rb, tail_off):
    j = pl.program_id(1)
    halo = 32

    @pl.when(j == 0)
    def _():
        xb_ref[0:halo, :] = jnp.zeros((halo, B_WIDTH), F32)

    xb_ref[halo:halo + rb, :] = a_ref[...] * _sigmoid(g_ref[...])
    span = rb + halo - SUBLANES
    for s in range(1, SUBLANES):
        xs_ref[s - 1, 0:span, :] = xb_ref[pl.ds(s, span), :]
    rc = 64
    for c in range(B_WIDTH // LANES):
        cs = slice(c * LANES, (c + 1) * LANES)
        wc = w_ref[:, cs]
        for r in range(rb // rc):
            acc = jnp.zeros((rc, LANES), F32) + cb_ref[:, cs]
            for t in range(CONV_W):
                off = halo - (CONV_W - 1) + t
                s, base = off % SUBLANES, (off // SUBLANES) * SUBLANES + r * rc
                win = xb_ref[base:base + rc, cs] if s == 0 else xs_ref[s - 1, base:base + rc, cs]
                acc = acc + win * wc[t:t + 1, :]
            y_ref[r * rc:(r + 1) * rc, cs] = acc
    o_ref[...] = _conv_tail(y_ref[...], bz_ref[...], lng_ref[...], lnb_ref[...], pw_ref)

    @pl.when(j == pl.num_programs(1) - 1)
    def _():
        st_ref[...] = xb_ref[pl.ds(tail_off, CONV_W - 1), :]

    xb_ref[0:halo, :] = xb_ref[rb:rb + halo, :]


def _conv_prompt(u, cw, cb, lng, lnb, pw, *, nb, t_pad, t_valid):
    rb = 128
    nj = t_pad // rb
    tail_off = (t_valid - (CONV_W - 1)) - ((nj - 1) * rb - 32)
    assert 0 <= tail_off and tail_off + CONV_W - 1 <= rb + 32

    def ublk(blk):
        return pl.BlockSpec((rb, 512), lambda b, j: (b * nj + j, blk))

    def full(a):
        return pl.BlockSpec(a.shape, lambda b, j: (0,) * a.ndim)

    return pl.pallas_call(
        functools.partial(_conv_prompt_kernel, rb=rb, tail_off=tail_off),
        grid=(nb, nj),
        in_specs=[ublk(BLK_GLUA), ublk(BLK_GLUG), ublk(BLK_BZ), full(cw), full(cb), full(lng), full(lnb), full(pw)],
        out_specs=[pl.BlockSpec((rb, B_WIDTH), lambda b, j: (b * nj + j, 0)),
                   pl.BlockSpec((None, CONV_W - 1, B_WIDTH), lambda b, j: (b, 0, 0))],
        out_shape=[jax.ShapeDtypeStruct((nb * t_pad, B_WIDTH), F32),
                   jax.ShapeDtypeStruct((nb, CONV_W - 1, B_WIDTH), F32)],
        scratch_shapes=[pltpu.VMEM((rb + 32, B_WIDTH), F32), pltpu.VMEM((rb, B_WIDTH), F32),
                        pltpu.VMEM((SUBLANES - 1, rb + 32, B_WIDTH), F32)],
        compiler_params=_cparams(("parallel", "arbitrary")),
        name="conv_prompt",
    )(u, u, u, cw, cb, lng, lnb, pw)


def _conv_decode_kernel(a_ref, g_ref, bz_ref, sin_ref, w_ref, cb_ref, lng_ref, lnb_ref, pw_ref, o_ref, sout_ref, xs_ref, *, ns, ds):
    nbuf = CONV_W - 1
    xs_ref[:, 0:nbuf, :] = sin_ref[...]
    glu = a_ref[...] * _sigmoid(g_ref[...])
    xs_ref[:, nbuf:nbuf + ds, :] = glu.reshape(ns, ds, B_WIDTH)
    acc = jnp.zeros((ns, ds, B_WIDTH), F32) + cb_ref[...]
    for t in range(CONV_W):
        acc = acc + xs_ref[:, pl.ds(t, ds), :] * w_ref[t:t + 1, :]
    o_ref[...] = _conv_tail(acc.reshape(ns * ds, B_WIDTH), bz_ref[...], lng_ref[...], lnb_ref[...], pw_ref)
    sout_ref[...] = xs_ref[:, pl.ds(ds, nbuf), :]


def _conv_decode(u, state, cw, cb, lng, lnb, pw, *, layer, nb, ds):
    ns = next(t for t in (16, 8, 4, 2, 1) if nb % t == 0)
    rb = ns * ds

    def ublk(blk):
        return pl.BlockSpec((rb, 512), lambda i: (i, blk))

    def full(a):
        return pl.BlockSpec(a.shape, lambda i: (0,) * a.ndim)

    return pl.pallas_call(
        functools.partial(_conv_decode_kernel, ns=ns, ds=ds),
        grid=(nb // ns,),
        in_specs=[ublk(BLK_GLUA), ublk(BLK_GLUG), ublk(BLK_BZ),
                  pl.BlockSpec((None, ns, CONV_W - 1, B_WIDTH), lambda i: (layer, i, 0, 0)),
                  full(cw), full(cb), full(lng), full(lnb), full(pw)],
        out_specs=[pl.BlockSpec((rb, B_WIDTH), lambda i: (i, 0)),
                   pl.BlockSpec((ns, CONV_W - 1, B_WIDTH), lambda i: (i, 0, 0))],
        out_shape=[jax.ShapeDtypeStruct((nb * ds, B_WIDTH), F32),
                   jax.ShapeDtypeStruct((nb, CONV_W - 1, B_WIDTH), F32)],
        scratch_shapes=[pltpu.VMEM((ns, CONV_W - 1 + ds, B_WIDTH), F32)],
        compiler_params=_cparams(("parallel",)),
        name="conv_decode",
    )(u, u, u, state, cw, cb, lng, lnb, pw)


def _dsa_prompt_kernel(q_ref, qi_ref, cz_ref, kv_ref, kiw_ref, o_ref, k16, vt16, ki16, ik, hi16, lo16, s_buf, acc_s, *, n_sel, nblk):
    i = pl.program_id(1)
    blk = LANES
    hd = C_HEAD_DIM
    nb1 = nblk + 1
    vregs = blk // SUBLANES

    @pl.when(i == 0)
    def _():
        for kb in range(nblk):
            sl = slice(kb * blk, (kb + 1) * blk)
            x = kv_ref[sl, :]
            for hk in range(C_KV_HEADS):
                k16[hk * nb1 + kb] = x[:, hk * hd:(hk + 1) * hd].astype(BF16)
            vt16[kb] = x[:, blk:2 * blk].T.astype(BF16)
            ki16[kb] = kiw_ref[sl, :][:, 0:IDX_DIM].astype(BF16)
        for hk in range(C_KV_HEADS):
            k16[hk * nb1 + nblk] = jnp.zeros((blk, hd), BF16)
        vt16[nblk] = jnp.zeros((blk, blk), BF16)
        ki16[nblk] = jnp.zeros((blk, IDX_DIM), BF16)

    nkb = i + 1
    npair = (nkb + 1) // 2
    r0 = pl.multiple_of(i * blk, blk)
    w_t = kiw_ref[pl.ds(r0, blk), :].T
    qi = qi_ref[...].astype(BF16)
    rowi = lax.broadcasted_iota(I32, (blk, blk), 0)
    lanei = lax.broadcasted_iota(I32, (blk, blk), 1)
    qpos = r0 + lanei

    qis = jnp.concatenate([qi[:, h * IDX_DIM:(h + 1) * IDX_DIM] for h in range(IDX_HEADS)], axis=0)
    w_h = [w_t[IDX_DIM + h:IDX_DIM + h + 1, :] * IDX_SCALE for h in range(IDX_HEADS)]

    def p1(pp, c):
        for u in range(2):
            kb = 2 * pp + u
            s = _nt(ki16[kb], qis)
            acc = jnp.zeros((blk, blk), F32)
            for h in range(IDX_HEADS):
                acc = acc + jnp.maximum(s[:, h * blk:(h + 1) * blk], 0.0) * w_h[h]
            allowed = (kb * blk + rowi) <= qpos
            key = jnp.where(allowed, _sort_key(acc), jnp.int32(INT_MIN))
            ik[kb] = key
            hi16[kb] = (key >> 16).astype(I16)
        return c

    lax.fori_loop(0, npair, p1, 0)

    one16 = jnp.ones((), BF16)
    zero16 = jnp.zeros((), BF16)

    def count16(ref, pred):
        def body(pp, c):
            parts = []
            for u in range(2):
                m = jnp.where(pred(ref[2 * pp + u]), one16, zero16).reshape(vregs // 2, 2 * SUBLANES, blk)
                parts += [m[j] for j in range(vregs // 2)]
            while len(parts) > 1:
                parts = [parts[j] + parts[j + 1] for j in range(0, len(parts), 2)]
            return c + parts[0]
        c = lax.fori_loop(0, npair, body, jnp.zeros((2 * SUBLANES, blk), BF16))
        return jnp.sum(c.astype(F32), axis=0, keepdims=True)

    def search16(ref, want):
        lo = jnp.full((1, blk), -(1 << 15), I32)
        zero = jnp.zeros((1, blk), I32)
        v = jnp.where(count16(ref, lambda k: k >= zero.astype(I16)) >= want, zero, lo)

        def bit_body(t, v):
            cand = v + lax.shift_left(jnp.int32(1), jnp.int32(14) - t)
            return jnp.where(count16(ref, lambda k: k >= cand.astype(I16)) >= want, cand, v)

        return lax.fori_loop(0, 15, bit_body, v)

    thr_hi = search16(hi16, jnp.float32(n_sel))
    thr_hi16 = thr_hi.astype(I16)
    want_lo = n_sel - count16(hi16, lambda k: k > thr_hi16)

    def p2(pp, c):
        for u in range(2):
            kb = 2 * pp + u
            low = (ik[kb] & jnp.int32(0xFFFF)) - jnp.int32(1 << 15)
            lo16[kb] = jnp.where((ik[kb] >> 16) == thr_hi, low, jnp.int32(-(1 << 15))).astype(I16)
        return c

    lax.fori_loop(0, npair, p2, 0)
    thr_lo = search16(lo16, want_lo)
    thr = thr_hi * jnp.int32(1 << 16) + (thr_lo + jnp.int32(1 << 15))

    def count32(pred):
        def body(pp, c):
            for u in range(2):
                m = jnp.where(pred(ik[2 * pp + u]), 1, 0)
                c = c + jnp.sum(m.reshape(vregs, SUBLANES, blk), axis=0)
            return c
        c = lax.fori_loop(0, npair, body, jnp.zeros((SUBLANES, blk), I32))
        return jnp.sum(c, axis=0, keepdims=True)

    need = (n_sel - count32(lambda k: k > thr)).astype(F32)

    ltri = (rowi >= lanei).astype(BF16)
    q = q_ref[...].astype(BF16)
    groups = C_HEADS // C_KV_HEADS
    gw = groups * blk
    qg = [jnp.concatenate([q[:, (hk * groups + g) * hd:(hk * groups + g + 1) * hd] for g in range(groups)], axis=0)
          for hk in range(C_KV_HEADS)]

    def p3(pp, carry):
        tcar = carry[0]
        mx = list(carry[1:])
        for u in range(2):
            kb = 2 * pp + u
            key = ik[kb]
            tie = key == thr
            pre = jnp.dot(ltri, jnp.where(tie, 1.0, 0.0).astype(BF16), preferred_element_type=F32) + tcar
            allowed = (kb * blk + rowi) <= qpos
            sel = ((tie & (pre <= need)) | (key > thr)) & allowed
            bias = jnp.where(sel, 0.0, MASK_NEG)
            tcar = pre[blk - 1:blk, :]
            bias_g = jnp.concatenate([bias] * groups, axis=1)
            for hk in range(C_KV_HEADS):
                s = _nt(k16[hk * nb1 + kb], qg[hk]) + bias_g
                s_buf[hk * nb1 + kb] = s
                mx[hk] = jnp.maximum(mx[hk], jnp.max(s.reshape(vregs, SUBLANES, gw), axis=0))
        return (tcar, *mx)

    init = (jnp.zeros((1, blk), F32),) + tuple(jnp.full((SUBLANES, gw), -3.0e38, F32) for _ in range(C_KV_HEADS))
    res = lax.fori_loop(0, npair, p3, init)
    m = [jnp.max(res[1 + hk], axis=0, keepdims=True) for hk in range(C_KV_HEADS)]

    acc_s[...] = jnp.zeros((C_KV_HEADS, hd, gw), F32)

    def p4(pp, carry):
        ls = list(carry)
        for u in range(2):
            kb = 2 * pp + u
            vt = vt16[kb]
            for hk in range(C_KV_HEADS):
                p = jnp.exp(s_buf[hk * nb1 + kb] - m[hk])
                ls[hk] = ls[hk] + jnp.sum(p.reshape(vregs, SUBLANES, gw), axis=0)
                acc_s[hk] = acc_s[hk] + jnp.dot(vt[hk * hd:(hk + 1) * hd, :], p.astype(BF16), preferred_element_type=F32)
        return tuple(ls)

    ls = lax.fori_loop(0, npair, p4, tuple(jnp.zeros((SUBLANES, gw), F32) for _ in range(C_KV_HEADS)))
    outs = []
    for hk in range(C_KV_HEADS):
        og = acc_s[hk] / jnp.sum(ls[hk], axis=0, keepdims=True)
        outs += [og[:, g * blk:(g + 1) * blk] for g in range(groups)]
    o_t = jnp.concatenate(outs, axis=0)
    o = jnp.concatenate([o_t[j * blk:(j + 1) * blk, :].T for j in range(C_WIDTH // blk)], axis=1)
    cz = cz_ref[...]
    o_ref[...] = o * (cz * _sigmoid(cz))


def _dsa_prompt(u, qr, qir, kv, kiw, *, nb, t_pad, n_sel):
    blk = LANES
    nblk = t_pad // blk
    gw = (C_HEADS // C_KV_HEADS) * blk
    return pl.pallas_call(
        functools.partial(_dsa_prompt_kernel, n_sel=n_sel, nblk=nblk),
        grid=(nb, nblk),
        in_specs=[
            pl.BlockSpec((blk, 512), lambda b, i: (b * nblk + i, 0)),
            pl.BlockSpec((blk, 256), lambda b, i: (b * nblk + i, 0)),
            pl.BlockSpec((blk, 512), lambda b, i: (b * nblk + i, BLK_CZ)),
            pl.BlockSpec((t_pad, 256), lambda b, i: (b, 0)),
            pl.BlockSpec((t_pad, LANES), lambda b, i: (b, 0)),
        ],
        out_specs=pl.BlockSpec((blk, C_WIDTH), lambda b, i: (b * nblk + i, 0)),
        out_shape=jax.ShapeDtypeStruct((nb * t_pad, C_WIDTH), F32),
        scratch_shapes=[
            pltpu.VMEM((C_KV_HEADS * (nblk + 1), blk, C_HEAD_DIM), BF16),
            pltpu.VMEM((nblk + 1, blk, blk), BF16),
            pltpu.VMEM((nblk + 1, blk, IDX_DIM), BF16),
            pltpu.VMEM((nblk + 1, blk, blk), I32),
            pltpu.VMEM((nblk + 1, blk, blk), I16),
            pltpu.VMEM((nblk + 1, blk, blk), I16),
            pltpu.VMEM((C_KV_HEADS * (nblk + 1), blk, gw), F32),
            pltpu.VMEM((C_KV_HEADS, C_HEAD_DIM, gw), F32),
        ],
        compiler_params=_cparams(("parallel", "arbitrary")),
        name="dsa_prompt",
    )(qr, qir, u, kv, kiw)


def _dsa_decode_kernel(pt_ref, q_ref, qi_ref, cz_ref, kvn_ref, kiwn_ref, *rest, n_sel, npages, ds):
    kpages = rest[0:npages]
    vpages = rest[npages:2 * npages]
    ipages = rest[2 * npages:3 * npages]
    o_ref = rest[3 * npages]
    ikey_ref = rest[3 * npages + 1]
    del pt_ref
    blk = LANES
    hd = C_HEAD_DIM
    nblk = npages + 1
    groups = C_HEADS // C_KV_HEADS

    kiwn = kiwn_ref[...]
    kvn = kvn_ref[...]
    qi = qi_ref[...]
    qis = jnp.concatenate([qi[:, h * IDX_DIM:(h + 1) * IDX_DIM] for h in range(IDX_HEADS)], axis=0).astype(BF16)
    wcol = jnp.concatenate([kiwn[:, IDX_DIM + h:IDX_DIM + h + 1] for h in range(IDX_HEADS)], axis=0) * IDX_SCALE
    zpad = jnp.zeros((blk - ds, blk), F32)
    knew = jnp.concatenate([kvn[:, 0:blk], zpad], axis=0).T
    vnew = jnp.concatenate([kvn[:, blk:2 * blk], zpad], axis=0).T
    inew = jnp.concatenate([kiwn, zpad], axis=0).T[0:IDX_DIM, :]
    rowq = lax.broadcasted_iota(I32, (ds, blk), 0)
    lane = lax.broadcasted_iota(I32, (ds, blk), 1)
    allowed_new = lane <= rowq

    def iscore(kpage):
        r = jnp.maximum(jnp.dot(qis, kpage.astype(BF16), preferred_element_type=F32), 0.0) * wcol
        acc = r[0:ds]
        for h in range(1, IDX_HEADS):
            acc = acc + r[h * ds:(h + 1) * ds]
        return _sort_key(jnp.zeros((ds, blk), F32) + acc)

    for j in range(npages):
        ikey_ref[:, j * blk:(j + 1) * blk] = iscore(ipages[j][...])
    ikey_ref[:, npages * blk:nblk * blk] = jnp.where(allowed_new, iscore(inew), jnp.int32(INT_MIN))
    keys = ikey_ref[...]

    def count_ge(cand):
        return jnp.sum(jnp.where(keys >= cand, 1, 0), axis=1, keepdims=True)

    thr = jnp.full((ds, 1), INT_MIN, I32)
    for step in range(8):
        shift = 28 - 4 * step
        digit = jnp.zeros((ds, 1), I32)
        for c in range(1, 16):
            inc = int(np.array(c << shift, dtype=np.uint32).astype(np.int32))
            digit = digit + jnp.where(count_ge(thr + jnp.int32(inc)) >= n_sel, 1, 0)
        thr = thr + digit * jnp.int32(1 << shift)

    gt = keys > thr
    tie = keys == thr
    need = (n_sel - jnp.sum(jnp.where(gt, 1, 0), axis=1, keepdims=True)).astype(F32)
    tief = jnp.where(tie, 1.0, 0.0)
    ri = lax.broadcasted_iota(I32, (blk, blk), 0)
    ci = lax.broadcasted_iota(I32, (blk, blk), 1)
    utri = (ri <= ci).astype(F32)
    carry = jnp.zeros((ds, 1), F32)
    biases = []
    for j in range(nblk):
        sl = slice(j * blk, (j + 1) * blk)
        pre = jnp.dot(tief[:, sl], utri, preferred_element_type=F32) + carry
        sel = (tie[:, sl] & (pre <= need)) | gt[:, sl]
        if j == npages:
            sel = sel & allowed_new
        biases.append(jnp.where(sel, 0.0, MASK_NEG))
        carry = pre[:, blk - 1:blk]
    bias = jnp.concatenate(biases, axis=1)
    bias_g = jnp.concatenate([bias] * groups, axis=0)

    q = q_ref[...]
    outs = [None] * C_HEADS
    for hk in range(C_KV_HEADS):
        qg = jnp.concatenate([q[:, (hk * groups + g) * hd:(hk * groups + g + 1) * hd] for g in range(groups)], axis=0).astype(BF16)
        ksl = slice(hk * hd, (hk + 1) * hd)
        ls = [jnp.dot(qg, kpages[j][ksl, :].astype(BF16), preferred_element_type=F32) for j in range(npages)]
        ls.append(jnp.dot(qg, knew[ksl, :].astype(BF16), preferred_element_type=F32))
        s = jnp.concatenate(ls, axis=1) + bias_g
        m = jnp.max(s, axis=1, keepdims=True)
        p = jnp.exp(s - m)
        l = jnp.sum(p, axis=1, keepdims=True)
        pb = p.astype(BF16)
        acc = _nt(pb[:, npages * blk:nblk * blk], vnew[ksl, :].astype(BF16))
        for j in range(npages):
            acc = acc + _nt(pb[:, j * blk:(j + 1) * blk], vpages[j][ksl, :].astype(BF16))
        og = acc / l
        for g in range(groups):
            outs[hk * groups + g] = og[g * ds:(g + 1) * ds, :]
    o = jnp.concatenate(outs, axis=1)
    cz = cz_ref[...]
    o_ref[...] = o * (cz * _sigmoid(cz))


def _dsa_decode(u, qr, qir, kv, kiw, cache_k, cache_v, cache_kidx, page_table, *, layer, nb, ds, n_sel):
    npages = page_table.shape[1]
    pt = page_table.reshape(-1)

    def page(j, width):
        return pl.BlockSpec((None, None, width, PAGE_SIZE), lambda b, pt_ref: (layer, pt_ref[b * npages + j], 0, 0))

    in_specs = [
        pl.BlockSpec((ds, 512), lambda b, pt_ref: (b, 0)),
        pl.BlockSpec((ds, 256), lambda b, pt_ref: (b, 0)),
        pl.BlockSpec((ds, 512), lambda b, pt_ref: (b, BLK_CZ)),
        pl.BlockSpec((ds, 256), lambda b, pt_ref: (b, 0)),
        pl.BlockSpec((ds, LANES), lambda b, pt_ref: (b, 0)),
    ]
    in_specs += [page(j, LANES) for j in range(npages)]
    in_specs += [page(j, LANES) for j in range(npages)]
    in_specs += [page(j, IDX_DIM) for j in range(npages)]
    grid_spec = pltpu.PrefetchScalarGridSpec(
        num_scalar_prefetch=1,
        grid=(nb,),
        in_specs=in_specs,
        out_specs=pl.BlockSpec((ds, C_WIDTH), lambda b, pt_ref: (b, 0)),
        scratch_shapes=[pltpu.VMEM((ds, (npages + 1) * LANES), I32)],
    )
    args = [pt, qr, qir, u, kv, kiw] + [cache_k] * npages + [cache_v] * npages + [cache_kidx] * npages
    return pl.pallas_call(
        functools.partial(_dsa_decode_kernel, n_sel=n_sel, npages=npages, ds=ds),
        grid_spec=grid_spec,
        out_shape=jax.ShapeDtypeStruct((nb * ds, C_WIDTH), F32),
        compiler_params=_cparams(("arbitrary",)),
        name="dsa_decode",
    )(*args)


def _merge_kernel(h_ref, ya_ref, yb_ref, yc_ref, ga_ref, gb_ref, gc_ref, wpa_ref, wpb_ref, wpc_ref, wo_ref, fg_ref, o_ref, *, final):
    def proj(y_ref, w_ref):
        return jnp.dot(y_ref[...].astype(BF16), w_ref[...], preferred_element_type=F32)

    m = (_sigmoid(ga_ref[...]) * proj(ya_ref, wpa_ref)
         + _sigmoid(gb_ref[...]) * proj(yb_ref, wpb_ref)
         + _sigmoid(gc_ref[...]) * proj(yc_ref, wpc_ref))
    h = h_ref[...] + jnp.dot(m.astype(BF16), wo_ref[...], preferred_element_type=F32)
    if final:
        h = h * lax.rsqrt(jnp.mean(h * h, axis=-1, keepdims=True) + EPS) * fg_ref[...]
    o_ref[...] = h


def _merge(h, ya, yb, yc, u, wpa, wpb, wpc, wo, fg, *, final):
    rows = h.shape[0]
    tm = next(t for t in (512, 256, 128, 64, 32, 16, 8) if rows % t == 0)

    def rowblk(width, blk=0):
        return pl.BlockSpec((tm, width), lambda i: (i, blk))

    def full(a):
        return pl.BlockSpec(a.shape, lambda i: (0,) * a.ndim)

    return pl.pallas_call(
        functools.partial(_merge_kernel, final=final),
        grid=(rows // tm,),
        in_specs=[rowblk(D_MODEL), rowblk(512), rowblk(512), rowblk(512),
                  rowblk(D_MODEL, BLK_GATE), rowblk(D_MODEL, BLK_GATE + 1), rowblk(D_MODEL, BLK_GATE + 2),
                  full(wpa), full(wpb), full(wpc), full(wo), full(fg)],
        out_specs=rowblk(D_MODEL),
        out_shape=jax.ShapeDtypeStruct((rows, D_MODEL), F32),
        compiler_params=_cparams(("parallel",)),
        name="merge",
    )(h, ya, yb, yc, u, u, u, wpa, wpb, wpc, wo, fg)


def _rope_tables(pos):
    half = ROT_DIM // 2
    inv = ROPE_THETA ** (-jnp.arange(half, dtype=F32) * 2.0 / ROT_DIM)
    ang = pos.astype(F32)[:, None] * jnp.tile(inv, LANES // half)[None, :]
    return jnp.cos(ang), jnp.sin(ang)


def _pack_cols(w):
    split = 4608
    tail = split + 68
    pad = jnp.zeros(w.shape[:-1] + (U_W - N_IN,), w.dtype)
    return jnp.concatenate([w[..., :split], w[..., tail:], w[..., split:tail], pad], axis=-1)


def kernel(x_prompt, x_sample, cache_k, cache_v, cache_kidx, state_hgrn, state_conv, page_table, meta_tokens, norm_g, w_in, b_in, lb_logits, hgrn_norm_g, conv_w, conv_b, conv_ln_g, conv_ln_b, conv_pw, w_pa, w_pb, w_pc, w_out, final_norm_g):
    nbp, seq, _ = x_prompt.shape
    nbs, ds, _ = x_sample.shape
    depth = w_in.shape[0]
    npages = page_table.shape[1]
    past = npages * PAGE_SIZE
    t_valid = seq + N_META
    t_pad = -(-t_valid // LANES) * LANES
    n_sel_p = min(TOPK_MAX, t_valid // 4)
    n_sel_s = min(TOPK_MAX, (past + ds) // 4)
    n_phys = cache_k.shape[1]

    meta = jnp.broadcast_to(meta_tokens[None].astype(F32), (nbp, N_META, D_MODEL))
    hp = jnp.concatenate([meta, x_prompt, jnp.zeros((nbp, t_pad - t_valid, D_MODEL), F32)], axis=1).reshape(nbp * t_pad, D_MODEL)
    hs = x_sample.reshape(nbs * ds, D_MODEL)
    cos_p, sin_p = _rope_tables(jnp.tile(jnp.arange(t_pad), nbp))
    cos_s, sin_s = _rope_tables(jnp.tile(past + jnp.arange(ds), nbs))

    w_in_p = _pack_cols(w_in).astype(BF16)
    b_in_p = _pack_cols(b_in).reshape(depth, 1, U_W)
    ck4 = cache_k.transpose(0, 1, 3, 4, 2).reshape(depth, n_phys, C_KV_HEADS * C_HEAD_DIM, PAGE_SIZE)
    cv4 = cache_v.transpose(0, 1, 3, 4, 2).reshape(depth, n_phys, C_KV_HEADS * C_HEAD_DIM, PAGE_SIZE)
    ci4 = cache_kidx.transpose(0, 1, 3, 2)
    fg = final_norm_g.reshape(1, D_MODEL)

    outs = {k: [] for k in ("pk", "pv", "pki", "ps", "pc", "sk", "sv", "ski", "ss", "sc")}
    for l in range(depth):
        g = norm_g[l].reshape(1, D_MODEL)
        gn = hgrn_norm_g[l].reshape(1, A_HEAD)
        cw, cb = conv_w[l], conv_b[l].reshape(1, B_WIDTH)
        lng, lnb = conv_ln_g[l].reshape(1, B_WIDTH), conv_ln_b[l].reshape(1, B_WIDTH)
        pw = conv_pw[l].astype(BF16)
        wpa, wpb, wpc, wo = (w[l].astype(BF16) for w in (w_pa, w_pb, w_pc, w_out))
        final = l == depth - 1

        u = _inproj(hp, g, w_in_p[l], b_in_p[l])
        qr, qir, kv, kiw = _rope(u, cos_p, sin_p)
        ya, s_new = _hgrn(u, lb_logits, gn, None, layer=l, nb=nbp, t_pad=t_pad, t_valid=t_valid, rb=128, sb=16)
        yb, c_new = _conv_prompt(u, cw, cb, lng, lnb, pw, nb=nbp, t_pad=t_pad, t_valid=t_valid)
        yc = _dsa_prompt(u, qr, qir, kv, kiw, nb=nbp, t_pad=t_pad, n_sel=n_sel_p)
        hp = _merge(hp, ya, yb, yc, u, wpa, wpb, wpc, wo, fg, final=final)
        kv3 = kv.reshape(nbp, t_pad, 2 * LANES)[:, :t_valid]
        outs["pk"].append(kv3[..., :LANES].reshape(nbp, t_valid, C_KV_HEADS, C_HEAD_DIM))
        outs["pv"].append(kv3[..., LANES:].reshape(nbp, t_valid, C_KV_HEADS, C_HEAD_DIM))
        outs["pki"].append(kiw.reshape(nbp, t_pad, LANES)[:, :t_valid, :IDX_DIM])
        outs["ps"].append(s_new)
        outs["pc"].append(c_new)

        u = _inproj(hs, g, w_in_p[l], b_in_p[l])
        qr, qir, kv, kiw = _rope(u, cos_s, sin_s)
        ya, s_new = _hgrn(u, lb_logits, gn, state_hgrn, layer=l, nb=nbs, t_pad=ds, t_valid=ds, rb=ds, sb=ds, state_layer=l)
        yb, c_new = _conv_decode(u, state_conv, cw, cb, lng, lnb, pw, layer=l, nb=nbs, ds=ds)
        yc = _dsa_decode(u, qr, qir, kv, kiw, ck4, cv4, ci4, page_table, layer=l, nb=nbs, ds=ds, n_sel=n_sel_s)
        hs = _merge(hs, ya, yb, yc, u, wpa, wpb, wpc, wo, fg, final=final)
        kv3 = kv.reshape(nbs, ds, 2 * LANES)
        outs["sk"].append(kv3[..., :LANES].reshape(nbs, ds, C_KV_HEADS, C_HEAD_DIM))
        outs["sv"].append(kv3[..., LANES:].reshape(nbs, ds, C_KV_HEADS, C_HEAD_DIM))
        outs["ski"].append(kiw.reshape(nbs, ds, LANES)[..., :IDX_DIM])
        outs["ss"].append(s_new)
        outs["sc"].append(c_new)

    y_prompt = hp.reshape(nbp, t_pad, D_MODEL)[:, N_META:t_valid]
    y_sample = hs.reshape(nbs, ds, D_MODEL)
    st = {k: jnp.stack(v) for k, v in outs.items()}
    return (y_prompt, y_sample, st["pk"], st["pv"], st["pki"], st["ps"], st["pc"],
            st["sk"], st["sv"], st["ski"], st["ss"], st["sc"])
```

```python
import functools

import numpy as np
import jax
import jax.numpy as jnp
from jax import lax
from jax.experimental import pallas as pl
from jax.experimental.pallas import tpu as pltpu

F32 = jnp.float32
BF16 = jnp.bfloat16
I32 = jnp.int32
I16 = jnp.int16

D_MODEL = 1024
N_META = 16
EPS = 1e-6
MASK_NEG = -1e30
A_WIDTH = 512
A_HEAD = 128
A_HEADS = 4
B_WIDTH = 512
CONV_W = 31
C_HEADS = 8
C_HEAD_DIM = 64
C_WIDTH = 512
C_KV_HEADS = 2
IDX_HEADS = 4
IDX_DIM = 64
TOPK_MAX = 256
PAGE_SIZE = 128
ROPE_THETA = 500000.0
ROT_DIM = 16
IDX_SCALE = (IDX_HEADS * IDX_DIM) ** -0.5
INT_MIN = -(2 ** 31)

LANES = 128
SUBLANES = 8
VMEM_LIMIT = 48 * 1024 * 1024

N_IN = 8260
U_W = 8448
TN_IN = 768
BLK_AQ, BLK_AF, BLK_AI, BLK_AZ, BLK_GLUA, BLK_GLUG, BLK_BZ, BLK_CQ, BLK_KVQI, BLK_CZ = range(10)
BLK_GATE = 5
BLK_KIW = 64
BLK256_KV = 16
BLK256_CQI = 17


def _cparams(sem):
    return pltpu.CompilerParams(dimension_semantics=sem, vmem_limit_bytes=VMEM_LIMIT)


def _sigmoid(x):
    return 1.0 / (1.0 + jnp.exp(-x))


def _nt(a, b):
    return lax.dot_general(a, b, (((1,), (1,)), ((), ())), preferred_element_type=F32)


def _tn(a, b):
    return lax.dot_general(a, b, (((0,), (0,)), ((), ())), preferred_element_type=F32)


def _sort_key(x):
    bits = pltpu.bitcast(x, I32)
    return bits ^ ((bits >> 31) & jnp.int32(0x7FFFFFFF))


ROPE_TWO_HEADS = (28, 29, 30, 31, 32, 34, 35)
ROPE_ONE_HEAD = (BLK_KIW,)


def _inproj_kernel(x_ref, g_ref, w_ref, b_ref, cos_ref, sin_ref, o_ref, xn_ref):
    j = pl.program_id(1)

    @pl.when(j == 0)
    def _():
        x = x_ref[...]
        ms = jnp.mean(x * x, axis=-1, keepdims=True)
        xn_ref[...] = (x * lax.rsqrt(ms + EPS) * g_ref[...]).astype(BF16)

    o_ref[...] = jnp.dot(xn_ref[...], w_ref[...], preferred_element_type=F32) + b_ref[...]

    per_tile = TN_IN // LANES
    half = ROT_DIM // 2
    lane = lax.broadcasted_iota(I32, (1, LANES), 1)
    for tile in sorted({blk // per_tile for blk in ROPE_TWO_HEADS + ROPE_ONE_HEAD}):
        @pl.when(j == tile)
        def _(tile=tile):
            cosd = cos_ref[...]
            sind = sin_ref[...]
            for blk in ROPE_TWO_HEADS + ROPE_ONE_HEAD:
                if blk // per_tile != tile:
                    continue
                pos = lane % C_HEAD_DIM if blk in ROPE_TWO_HEADS else lane
                c = jnp.where(pos < ROT_DIM, cosd, 1.0)
                a = jnp.where((pos >= half) & (pos < ROT_DIM), sind, 0.0)
                b = jnp.where(pos < half, -sind, 0.0)
                sl = slice((blk % per_tile) * LANES, (blk % per_tile + 1) * LANES)
                x = o_ref[:, sl]
                o_ref[:, sl] = x * c + pltpu.roll(x, half, 1) * a + pltpu.roll(x, LANES - half, 1) * b


def _inproj(h, g, w, b, cosd, sind):
    rows = h.shape[0]
    tm = next(t for t in (1024, 512, 256, 128, 64, 32, 16, 8) if rows % t == 0)
    return pl.pallas_call(
        _inproj_kernel,
        grid=(rows // tm, U_W // TN_IN),
        in_specs=[
            pl.BlockSpec((tm, D_MODEL), lambda i, j: (i, 0)),
            pl.BlockSpec((1, D_MODEL), lambda i, j: (0, 0)),
            pl.BlockSpec((D_MODEL, TN_IN), lambda i, j: (0, j)),
            pl.BlockSpec((1, TN_IN), lambda i, j: (0, j)),
            pl.BlockSpec((tm, LANES), lambda i, j: (i, 0)),
            pl.BlockSpec((tm, LANES), lambda i, j: (i, 0)),
        ],
        out_specs=pl.BlockSpec((tm, TN_IN), lambda i, j: (i, j)),
        out_shape=jax.ShapeDtypeStruct((rows, U_W), F32),
        scratch_shapes=[pltpu.VMEM((tm, D_MODEL), BF16)],
        compiler_params=_cparams(("parallel", "arbitrary")),
        name="inproj",
    )(h, g, w, b, cosd, sind)


def _hgrn_kernel(*refs, layer, sb, rb, t_valid, has_s0):
    aq_ref, af_ref, ai_ref, az_ref, lbl_ref, gn_ref = refs[:6]
    if has_s0:
        s0_ref, o_ref, sout_ref, st_ref = refs[6:]
    else:
        o_ref, sout_ref, st_ref = refs[6:]
    j = pl.program_id(1)
    nsub_max = rb // sb

    @pl.when(j == 0)
    def _():
        for h in range(A_HEADS):
            if has_s0:
                st_ref[h] = s0_ref[h].T
            else:
                st_ref[h] = jnp.zeros((A_HEAD, A_HEAD), F32)

    lg = lbl_ref[...]
    e = jnp.exp(lg - jnp.max(lg, axis=0, keepdims=True))
    sm = e / jnp.sum(e, axis=0, keepdims=True)
    lb = jnp.zeros((1, A_WIDTH), F32)
    for i in range(1, layer + 1):
        lb = lb + sm[i:i + 1, :]

    n_sub = jnp.clip((t_valid - j * rb) // sb, 0, nsub_max)
    if nsub_max > 1:
        o_ref[...] = jnp.zeros((rb, A_WIDTH), F32)
    ri = lax.broadcasted_iota(I32, (sb, sb), 0)
    ci = lax.broadcasted_iota(I32, (sb, sb), 1)
    tri = (ri >= ci).astype(F32)
    rowi = lax.broadcasted_iota(I32, (sb, 1), 0)
    gn = gn_ref[...]

    def sub(i, carry):
        r0 = pl.multiple_of(i * sb, sb)
        f = af_ref[pl.ds(r0, sb), :]
        aq = aq_ref[pl.ds(r0, sb), :]
        v = ai_ref[pl.ds(r0, sb), :]
        az = az_ref[pl.ds(r0, sb), :]
        logf = jnp.log(lb + (1.0 - lb) * _sigmoid(f))
        kk = (1.0 - lb) * _sigmoid(-f)
        q = aq * _sigmoid(aq)
        b = jnp.dot(tri, logf, preferred_element_type=F32, precision=lax.Precision.HIGHEST)
        blast = b[sb - 1:sb, :]
        qe = q * jnp.exp(b)
        kd = kk * jnp.exp(blast - b)
        dec = jnp.exp(blast)
        outs = []
        for h in range(A_HEADS):
            sl = slice(h * A_HEAD, (h + 1) * A_HEAD)
            st = st_ref[h]
            o = _nt(qe[:, sl].astype(BF16), st.astype(BF16))
            qh, kh, vh, bh = q[:, sl], kk[:, sl], v[:, sl], b[:, sl]
            for s in range(sb):
                w = jnp.exp(bh - bh[s:s + 1, :])
                a = jnp.sum(qh * (kh[s:s + 1, :] * w), axis=-1, keepdims=True)
                o = o + jnp.where(rowi >= s, a, 0.0) * vh[s:s + 1, :]
            st_ref[h] = st * dec[:, sl] + _tn(vh.astype(BF16), kd[:, sl].astype(BF16))
            on = o * lax.rsqrt(jnp.mean(o * o, axis=-1, keepdims=True) + EPS) * gn
            outs.append(on)
        o_ref[pl.ds(r0, sb), :] = jnp.concatenate(outs, axis=1) * (az * _sigmoid(az))
        return carry

    lax.fori_loop(0, n_sub, sub, 0)

    @pl.when(j == pl.num_programs(1) - 1)
    def _():
        for h in range(A_HEADS):
            sout_ref[h] = st_ref[h].T


def _hgrn(u, lb_logits, gn_g, s0, *, layer, nb, t_pad, t_valid, rb, sb, state_layer=None):
    nj = t_pad // rb
    has_s0 = s0 is not None

    def ublk(blk):
        return pl.BlockSpec((rb, 512), lambda b, j: (b * nj + j, blk))

    in_specs = [ublk(BLK_AQ), ublk(BLK_AF), ublk(BLK_AI), ublk(BLK_AZ),
                pl.BlockSpec(lb_logits.shape, lambda b, j: (0, 0)),
                pl.BlockSpec((1, A_HEAD), lambda b, j: (0, 0))]
    args = [u, u, u, u, lb_logits, gn_g]
    if has_s0:
        in_specs.append(pl.BlockSpec((None, None, A_HEADS, A_HEAD, A_HEAD), lambda b, j: (state_layer, b, 0, 0, 0)))
        args.append(s0)
    return pl.pallas_call(
        functools.partial(_hgrn_kernel, layer=layer, sb=sb, rb=rb, t_valid=t_valid, has_s0=has_s0),
        grid=(nb, nj),
        in_specs=in_specs,
        out_specs=[pl.BlockSpec((rb, A_WIDTH), lambda b, j: (b * nj + j, 0)),
                   pl.BlockSpec((None, A_HEADS, A_HEAD, A_HEAD), lambda b, j: (b, 0, 0, 0))],
        out_shape=[jax.ShapeDtypeStruct((nb * t_pad, A_WIDTH), F32),
                   jax.ShapeDtypeStruct((nb, A_HEADS, A_HEAD, A_HEAD), F32)],
        scratch_shapes=[pltpu.VMEM((A_HEADS, A_HEAD, A_HEAD), F32)],
        compiler_params=_cparams(("parallel", "arbitrary")),
        name="hgrn",
    )(*args)


def _hgrn_levels(c):
    out, size = [], SUBLANES
    while size < c:
        out.append(size)
        size *= 2
    return out


def _hgrn_chunk_kernel(aq_ref, af_ref, ai_ref, az_ref, lbl_ref, gn_ref, o_ref, sout_ref,
                       st_ref, q_s, k_s, b_s, oi_s, dg_s, qf_s, kf_s, mask_s, *, layer, c, t_valid):
    j = pl.program_id(1)
    levels = _hgrn_levels(c)
    ri = lax.broadcasted_iota(I32, (c, c), 0)
    ci = lax.broadcasted_iota(I32, (c, c), 1)

    @pl.when(j == 0)
    def _():
        for h in range(A_HEADS):
            st_ref[h] = jnp.zeros((A_HEAD, A_HEAD), F32)
        for n, size in enumerate(levels):
            same = (ri // (2 * size)) == (ci // (2 * size))
            mask_s[n] = jnp.where(same & ((ri % (2 * size)) >= size) & ((ci % (2 * size)) < size), 1.0, 0.0)

    lg = lbl_ref[...]
    e = jnp.exp(lg - jnp.max(lg, axis=0, keepdims=True))
    sm = e / jnp.sum(e, axis=0, keepdims=True)
    lb = jnp.zeros((1, A_WIDTH), F32)
    for i in range(1, layer + 1):
        lb = lb + sm[i:i + 1, :]

    valid = (j * c + lax.broadcasted_iota(I32, (c, 1), 0)) < t_valid
    f = af_ref[...]
    aq = aq_ref[...]
    logf = jnp.where(valid, jnp.log(lb + (1.0 - lb) * _sigmoid(f)), 0.0)
    kk = jnp.where(valid, (1.0 - lb) * _sigmoid(-f), 0.0)
    q = aq * _sigmoid(aq)
    b = jnp.dot((ri >= ci).astype(F32), logf, preferred_element_type=F32, precision=lax.Precision.HIGHEST)
    blast = b[c - 1:c, :]
    q_s[...] = q
    k_s[...] = kk
    b_s[...] = b
    for n, size in enumerate(levels):
        groups = c // (2 * size)
        bref = jnp.broadcast_to(b.reshape(groups, 2 * size, A_WIDTH)[:, size - 1:size, :], (groups, 2 * size, A_WIDTH))
        fac = jnp.exp(-jnp.abs(b - bref.reshape(c, A_WIDTH)))
        qf_s[n] = (q * fac).astype(BF16)
        kf_s[n] = (kk * fac).astype(BF16)
    qe = (q * jnp.exp(b)).astype(BF16)
    kd = (kk * jnp.exp(blast - b)).astype(BF16)
    dec = jnp.exp(blast)
    vb = ai_ref[...].astype(BF16)
    for h in range(A_HEADS):
        sl = slice(h * A_HEAD, (h + 1) * A_HEAD)
        st = st_ref[h]
        o = _nt(qe[:, sl], st.astype(BF16))
        if levels:
            att = jnp.zeros((c, c), F32)
            for n in range(len(levels)):
                att = att + _nt(qf_s[n, :, sl], kf_s[n, :, sl]) * mask_s[n]
            o = o + jnp.dot(att.astype(BF16), vb[:, sl], preferred_element_type=F32)
        oi_s[:, sl] = o
        st_ref[h] = st * dec[:, sl] + _tn(vb[:, sl], kd[:, sl])

    row8 = lax.broadcasted_iota(I32, (SUBLANES, 1), 0)
    for x in range(c // SUBLANES):
        rows = slice(x * SUBLANES, (x + 1) * SUBLANES)
        outs = []
        for h in range(A_HEADS):
            sl = slice(h * A_HEAD, (h + 1) * A_HEAD)
            qh, kh, bh, vh = q_s[rows, sl], k_s[rows, sl], b_s[rows, sl], ai_ref[rows, sl]
            o = jnp.zeros((SUBLANES, A_HEAD), F32)
            for s in range(SUBLANES):
                w = jnp.exp(bh - bh[s:s + 1, :])
                a = jnp.sum(qh * (kh[s:s + 1, :] * w), axis=-1, keepdims=True)
                o = o + jnp.where(row8 >= s, a, 0.0) * vh[s:s + 1, :]
            outs.append(o)
        dg_s[rows, :] = jnp.concatenate(outs, axis=1)

    gn = gn_ref[...]
    az = az_ref[...]
    outs = []
    for h in range(A_HEADS):
        sl = slice(h * A_HEAD, (h + 1) * A_HEAD)
        o = oi_s[:, sl] + dg_s[:, sl]
        outs.append(o * lax.rsqrt(jnp.mean(o * o, axis=-1, keepdims=True) + EPS) * gn)
    o_ref[...] = jnp.concatenate(outs, axis=1) * (az * _sigmoid(az))

    @pl.when(j == pl.num_programs(1) - 1)
    def _():
        for h in range(A_HEADS):
            sout_ref[h] = st_ref[h].T


def _hgrn_chunked(u, lb_logits, gn_g, *, layer, nb, t_pad, t_valid):
    c = LANES
    nj = t_pad // c
    nlev = max(len(_hgrn_levels(c)), 1)

    def ublk(blk):
        return pl.BlockSpec((c, 512), lambda b, j: (b * nj + j, blk))

    return pl.pallas_call(
        functools.partial(_hgrn_chunk_kernel, layer=layer, c=c, t_valid=t_valid),
        grid=(nb, nj),
        in_specs=[ublk(BLK_AQ), ublk(BLK_AF), ublk(BLK_AI), ublk(BLK_AZ),
                  pl.BlockSpec(lb_logits.shape, lambda b, j: (0, 0)),
                  pl.BlockSpec((1, A_HEAD), lambda b, j: (0, 0))],
        out_specs=[pl.BlockSpec((c, A_WIDTH), lambda b, j: (b * nj + j, 0)),
                   pl.BlockSpec((None, A_HEADS, A_HEAD, A_HEAD), lambda b, j: (b, 0, 0, 0))],
        out_shape=[jax.ShapeDtypeStruct((nb * t_pad, A_WIDTH), F32),
                   jax.ShapeDtypeStruct((nb, A_HEADS, A_HEAD, A_HEAD), F32)],
        scratch_shapes=[pltpu.VMEM((A_HEADS, A_HEAD, A_HEAD), F32),
                        pltpu.VMEM((c, A_WIDTH), F32),
                        pltpu.VMEM((c, A_WIDTH), F32),
                        pltpu.VMEM((c, A_WIDTH), F32),
                        pltpu.VMEM((c, A_WIDTH), F32),
                        pltpu.VMEM((c, A_WIDTH), F32),
                        pltpu.VMEM((nlev, c, A_WIDTH), BF16),
                        pltpu.VMEM((nlev, c, A_WIDTH), BF16),
                        pltpu.VMEM((nlev, c, c), F32)],
        compiler_params=_cparams(("parallel", "arbitrary")),
        name="hgrn_chunked",
    )(u, u, u, u, lb_logits, gn_g)


def _conv_tail(y, bz, lng, lnb, pw_ref):
    mu = jnp.mean(y, axis=-1, keepdims=True)
    d = y - mu
    var = jnp.mean(d * d, axis=-1, keepdims=True)
    y = d * lax.rsqrt(var + EPS) * lng + lnb
    y = y * _sigmoid(y)
    z = jnp.dot(y.astype(BF16), pw_ref[...], preferred_element_type=F32)
    return z * (bz * _sigmoid(bz))


def _conv_prompt_kernel(a_ref, g_ref, bz_ref, w_ref, cb_ref, lng_ref, lnb_ref, pw_ref, o_ref, st_ref, xb_ref, y_ref, xs_ref, *, rb, tail_off):
    j = pl.program_id(1)
    halo = 32

    @pl.when(j == 0)
    def _():
        xb_ref[0:halo, :] = jnp.zeros((halo, B_WIDTH), F32)

    xb_ref[halo:halo + rb, :] = a_ref[...] * _sigmoid(g_ref[...])
    span = rb + halo - SUBLANES
    for s in range(1, SUBLANES):
        xs_ref[s - 1, 0:span, :] = xb_ref[pl.ds(s, span), :]
    rc = 64
    for c in range(B_WIDTH // LANES):
        cs = slice(c * LANES, (c + 1) * LANES)
        wc = w_ref[:, cs]
        for r in range(rb // rc):
            acc = jnp.zeros((rc, LANES), F32) + cb_ref[:, cs]
            for t in range(CONV_W):
                off = halo - (CONV_W - 1) + t
                s, base = off % SUBLANES, (off // SUBLANES) * SUBLANES + r * rc
                win = xb_ref[base:base + rc, cs] if s == 0 else xs_ref[s - 1, base:base + rc, cs]
                acc = acc + win * wc[t:t + 1, :]
            y_ref[r * rc:(r + 1) * rc, cs] = acc
    o_ref[...] = _conv_tail(y_ref[...], bz_ref[...], lng_ref[...], lnb_ref[...], pw_ref)

    @pl.when(j == pl.num_programs(1) - 1)
    def _():
        st_ref[...] = xb_ref[pl.ds(tail_off, CONV_W - 1), :]

    xb_ref[0:halo, :] = xb_ref[rb:rb + halo, :]


def _conv_prompt(u, cw, cb, lng, lnb, pw, *, nb, t_pad, t_valid):
    rb = 128
    nj = t_pad // rb
    tail_off = (t_valid - (CONV_W - 1)) - ((nj - 1) * rb - 32)
    assert 0 <= tail_off and tail_off + CONV_W - 1 <= rb + 32

    def ublk(blk):
        return pl.BlockSpec((rb, 512), lambda b, j: (b * nj + j, blk))

    def full(a):
        return pl.BlockSpec(a.shape, lambda b, j: (0,) * a.ndim)

    return pl.pallas_call(
        functools.partial(_conv_prompt_kernel, rb=rb, tail_off=tail_off),
        grid=(nb, nj),
        in_specs=[ublk(BLK_GLUA), ublk(BLK_GLUG), ublk(BLK_BZ), full(cw), full(cb), full(lng), full(lnb), full(pw)],
        out_specs=[pl.BlockSpec((rb, B_WIDTH), lambda b, j: (b * nj + j, 0)),
                   pl.BlockSpec((None, CONV_W - 1, B_WIDTH), lambda b, j: (b, 0, 0))],
        out_shape=[jax.ShapeDtypeStruct((nb * t_pad, B_WIDTH), F32),
                   jax.ShapeDtypeStruct((nb, CONV_W - 1, B_WIDTH), F32)],
        scratch_shapes=[pltpu.VMEM((rb + 32, B_WIDTH), F32), pltpu.VMEM((rb, B_WIDTH), F32),
                        pltpu.VMEM((SUBLANES - 1, rb + 32, B_WIDTH), F32)],
        compiler_params=_cparams(("parallel", "arbitrary")),
        name="conv_prompt",
    )(u, u, u, cw, cb, lng, lnb, pw)


def _conv_decode_kernel(a_ref, g_ref, bz_ref, sin_ref, w_ref, cb_ref, lng_ref, lnb_ref, pw_ref, o_ref, sout_ref, xs_ref, *, ns, ds):
    nbuf = CONV_W - 1
    xs_ref[:, 0:nbuf, :] = sin_ref[...]
    glu = a_ref[...] * _sigmoid(g_ref[...])
    xs_ref[:, nbuf:nbuf + ds, :] = glu.reshape(ns, ds, B_WIDTH)
    acc = jnp.zeros((ns, ds, B_WIDTH), F32) + cb_ref[...]
    for t in range(CONV_W):
        acc = acc + xs_ref[:, pl.ds(t, ds), :] * w_ref[t:t + 1, :]
    o_ref[...] = _conv_tail(acc.reshape(ns * ds, B_WIDTH), bz_ref[...], lng_ref[...], lnb_ref[...], pw_ref)
    sout_ref[...] = xs_ref[:, pl.ds(ds, nbuf), :]


def _conv_decode(u, state, cw, cb, lng, lnb, pw, *, layer, nb, ds):
    ns = next(t for t in (16, 8, 4, 2, 1) if nb % t == 0)
    rb = ns * ds

    def ublk(blk):
        return pl.BlockSpec((rb, 512), lambda i: (i, blk))

    def full(a):
        return pl.BlockSpec(a.shape, lambda i: (0,) * a.ndim)

    return pl.pallas_call(
        functools.partial(_conv_decode_kernel, ns=ns, ds=ds),
        grid=(nb // ns,),
        in_specs=[ublk(BLK_GLUA), ublk(BLK_GLUG), ublk(BLK_BZ),
                  pl.BlockSpec((None, ns, CONV_W - 1, B_WIDTH), lambda i: (layer, i, 0, 0)),
                  full(cw), full(cb), full(lng), full(lnb), full(pw)],
        out_specs=[pl.BlockSpec((rb, B_WIDTH), lambda i: (i, 0)),
                   pl.BlockSpec((ns, CONV_W - 1, B_WIDTH), lambda i: (i, 0, 0))],
        out_shape=[jax.ShapeDtypeStruct((nb * ds, B_WIDTH), F32),
                   jax.ShapeDtypeStruct((nb, CONV_W - 1, B_WIDTH), F32)],
        scratch_shapes=[pltpu.VMEM((ns, CONV_W - 1 + ds, B_WIDTH), F32)],
        compiler_params=_cparams(("parallel",)),
        name="conv_decode",
    )(u, u, u, state, cw, cb, lng, lnb, pw)


def _dsa_prompt_kernel(q_ref, qi_ref, cz_ref, kv_ref, kiw_ref, o_ref, k16, vt16, ki16, ik, hi16, lo16, s_buf, acc_s, *, n_sel, nblk):
    i = pl.program_id(1)
    blk = LANES
    hd = C_HEAD_DIM
    nb1 = nblk + 1
    vregs = blk // SUBLANES

    @pl.when(i == 0)
    def _():
        for kb in range(nblk):
            sl = slice(kb * blk, (kb + 1) * blk)
            x = kv_ref[sl, :]
            for hk in range(C_KV_HEADS):
                k16[hk * nb1 + kb] = x[:, hk * hd:(hk + 1) * hd].astype(BF16)
            vt16[kb] = x[:, blk:2 * blk].T.astype(BF16)
            ki16[kb] = kiw_ref[sl, :][:, 0:IDX_DIM].astype(BF16)
        for hk in range(C_KV_HEADS):
            k16[hk * nb1 + nblk] = jnp.zeros((blk, hd), BF16)
        vt16[nblk] = jnp.zeros((blk, blk), BF16)
        ki16[nblk] = jnp.zeros((blk, IDX_DIM), BF16)

    nkb = i + 1
    npair = (nkb + 1) // 2
    r0 = pl.multiple_of(i * blk, blk)
    w_t = kiw_ref[pl.ds(r0, blk), :].T
    qi = qi_ref[...].astype(BF16)
    rowi = lax.broadcasted_iota(I32, (blk, blk), 0)
    lanei = lax.broadcasted_iota(I32, (blk, blk), 1)
    qpos = r0 + lanei

    qis = jnp.concatenate([qi[:, h * IDX_DIM:(h + 1) * IDX_DIM] for h in range(IDX_HEADS)], axis=0)
    w_h = [w_t[IDX_DIM + h:IDX_DIM + h + 1, :] * IDX_SCALE for h in range(IDX_HEADS)]

    def p1(pp, c):
        for u in range(2):
            kb = 2 * pp + u
            s = _nt(ki16[kb], qis)
            acc = jnp.zeros((blk, blk), F32)
            for h in range(IDX_HEADS):
                acc = acc + jnp.maximum(s[:, h * blk:(h + 1) * blk], 0.0) * w_h[h]
            allowed = (kb * blk + rowi) <= qpos
            key = jnp.where(allowed, _sort_key(acc), jnp.int32(INT_MIN))
            ik[kb] = key
            hi16[kb] = (key >> 16).astype(I16)
        return c

    lax.fori_loop(0, npair, p1, 0)

    one16 = jnp.ones((), BF16)
    zero16 = jnp.zeros((), BF16)

    def count16(ref, pred):
        def body(pp, c):
            parts = []
            for u in range(2):
                m = jnp.where(pred(ref[2 * pp + u]), one16, zero16).reshape(vregs // 2, 2 * SUBLANES, blk)
                parts += [m[j] for j in range(vregs // 2)]
            while len(parts) > 1:
                parts = [parts[j] + parts[j + 1] for j in range(0, len(parts), 2)]
            return c + parts[0]
        c = lax.fori_loop(0, npair, body, jnp.zeros((2 * SUBLANES, blk), BF16))
        return jnp.sum(c.astype(F32), axis=0, keepdims=True)

    def search16(ref, want):
        lo = jnp.full((1, blk), -(1 << 15), I32)
        zero = jnp.zeros((1, blk), I32)
        v = jnp.where(count16(ref, lambda k: k >= zero.astype(I16)) >= want, zero, lo)

        def bit_body(t, v):
            cand = v + lax.shift_left(jnp.int32(1), jnp.int32(14) - t)
            return jnp.where(count16(ref, lambda k: k >= cand.astype(I16)) >= want, cand, v)

        return lax.fori_loop(0, 15, bit_body, v)

    thr_hi = search16(hi16, jnp.float32(n_sel))
    thr_hi16 = thr_hi.astype(I16)
    want_lo = n_sel - count16(hi16, lambda k: k > thr_hi16)

    def p2(pp, c):
        for u in range(2):
            kb = 2 * pp + u
            low = (ik[kb] & jnp.int32(0xFFFF)) - jnp.int32(1 << 15)
            lo16[kb] = jnp.where((ik[kb] >> 16) == thr_hi, low, jnp.int32(-(1 << 15))).astype(I16)
        return c

    lax.fori_loop(0, npair, p2, 0)
    thr_lo = search16(lo16, want_lo)
    thr = thr_hi * jnp.int32(1 << 16) + (thr_lo + jnp.int32(1 << 15))

    def count32(pred):
        def body(pp, c):
            for u in range(2):
                m = jnp.where(pred(ik[2 * pp + u]), 1, 0)
                c = c + jnp.sum(m.reshape(vregs, SUBLANES, blk), axis=0)
            return c
        c = lax.fori_loop(0, npair, body, jnp.zeros((SUBLANES, blk), I32))
        return jnp.sum(c, axis=0, keepdims=True)

    need = (n_sel - count32(lambda k: k > thr)).astype(F32)

    ltri = (rowi >= lanei).astype(BF16)
    q = (q_ref[...] * C_HEAD_DIM ** -0.5).astype(BF16)
    groups = C_HEADS // C_KV_HEADS
    gw = groups * blk
    qg = [jnp.concatenate([q[:, (hk * groups + g) * hd:(hk * groups + g + 1) * hd] for g in range(groups)], axis=0)
          for hk in range(C_KV_HEADS)]

    def p3(pp, carry):
        tcar = carry[0]
        mx = list(carry[1:])
        for u in range(2):
            kb = 2 * pp + u
            key = ik[kb]
            tie = key == thr
            pre = jnp.dot(ltri, jnp.where(tie, 1.0, 0.0).astype(BF16), preferred_element_type=F32) + tcar
            allowed = (kb * blk + rowi) <= qpos
            sel = ((tie & (pre <= need)) | (key > thr)) & allowed
            bias = jnp.where(sel, 0.0, MASK_NEG)
            tcar = pre[blk - 1:blk, :]
            bias_g = jnp.concatenate([bias] * groups, axis=1)
            for hk in range(C_KV_HEADS):
                s = _nt(k16[hk * nb1 + kb], qg[hk]) + bias_g
                s_buf[hk * nb1 + kb] = s
                mx[hk] = jnp.maximum(mx[hk], jnp.max(s.reshape(vregs, SUBLANES, gw), axis=0))
        return (tcar, *mx)

    init = (jnp.zeros((1, blk), F32),) + tuple(jnp.full((SUBLANES, gw), -3.0e38, F32) for _ in range(C_KV_HEADS))
    res = lax.fori_loop(0, npair, p3, init)
    m = [jnp.max(res[1 + hk], axis=0, keepdims=True) for hk in range(C_KV_HEADS)]

    acc_s[...] = jnp.zeros((C_KV_HEADS, hd, gw), F32)

    def p4(pp, carry):
        ls = list(carry)
        for u in range(2):
            kb = 2 * pp + u
            vt = vt16[kb]
            for hk in range(C_KV_HEADS):
                p = jnp.exp(s_buf[hk * nb1 + kb] - m[hk])
                ls[hk] = ls[hk] + jnp.sum(p.reshape(vregs, SUBLANES, gw), axis=0)
                acc_s[hk] = acc_s[hk] + jnp.dot(vt[hk * hd:(hk + 1) * hd, :], p.astype(BF16), preferred_element_type=F32)
        return tuple(ls)

    ls = lax.fori_loop(0, npair, p4, tuple(jnp.zeros((SUBLANES, gw), F32) for _ in range(C_KV_HEADS)))
    outs = []
    for hk in range(C_KV_HEADS):
        og = acc_s[hk] / jnp.sum(ls[hk], axis=0, keepdims=True)
        outs += [og[:, g * blk:(g + 1) * blk] for g in range(groups)]
    o_t = jnp.concatenate(outs, axis=0)
    o = jnp.concatenate([o_t[j * blk:(j + 1) * blk, :].T for j in range(C_WIDTH // blk)], axis=1)
    cz = cz_ref[...]
    o_ref[...] = o * (cz * _sigmoid(cz))


def _dsa_prompt(u, *, nb, t_pad, n_sel):
    blk = LANES
    nblk = t_pad // blk
    gw = (C_HEADS // C_KV_HEADS) * blk
    return pl.pallas_call(
        functools.partial(_dsa_prompt_kernel, n_sel=n_sel, nblk=nblk),
        grid=(nb, nblk),
        in_specs=[
            pl.BlockSpec((blk, 512), lambda b, i: (b * nblk + i, BLK_CQ)),
            pl.BlockSpec((blk, 256), lambda b, i: (b * nblk + i, BLK256_CQI)),
            pl.BlockSpec((blk, 512), lambda b, i: (b * nblk + i, BLK_CZ)),
            pl.BlockSpec((t_pad, 256), lambda b, i: (b, BLK256_KV)),
            pl.BlockSpec((t_pad, LANES), lambda b, i: (b, BLK_KIW)),
        ],
        out_specs=pl.BlockSpec((blk, C_WIDTH), lambda b, i: (b * nblk + i, 0)),
        out_shape=jax.ShapeDtypeStruct((nb * t_pad, C_WIDTH), F32),
        scratch_shapes=[
            pltpu.VMEM((C_KV_HEADS * (nblk + 1), blk, C_HEAD_DIM), BF16),
            pltpu.VMEM((nblk + 1, blk, blk), BF16),
            pltpu.VMEM((nblk + 1, blk, IDX_DIM), BF16),
            pltpu.VMEM((nblk + 1, blk, blk), I32),
            pltpu.VMEM((nblk + 1, blk, blk), I16),
            pltpu.VMEM((nblk + 1, blk, blk), I16),
            pltpu.VMEM((C_KV_HEADS * (nblk + 1), blk, gw), F32),
            pltpu.VMEM((C_KV_HEADS, C_HEAD_DIM, gw), F32),
        ],
        compiler_params=_cparams(("parallel", "arbitrary")),
        name="dsa_prompt",
    )(u, u, u, u, u)


def _dsa_decode_kernel(pt_ref, q_ref, qi_ref, cz_ref, kvn_ref, kiwn_ref, *rest, n_sel, npages, ds):
    del pt_ref
    kpages = rest[0:npages]
    vpages = rest[npages:2 * npages]
    ipages = rest[2 * npages:3 * npages]
    o_ref = rest[3 * npages]
    ikey_ref = rest[3 * npages + 1]
    q, qi, cz, kvn, kiwn = q_ref[...] * C_HEAD_DIM ** -0.5, qi_ref[...], cz_ref[...], kvn_ref[...], kiwn_ref[...]
    blk = LANES
    hd = C_HEAD_DIM
    nblk = npages + 1
    groups = C_HEADS // C_KV_HEADS

    qis = jnp.concatenate([qi[:, h * IDX_DIM:(h + 1) * IDX_DIM] for h in range(IDX_HEADS)], axis=0).astype(BF16)
    wcol = jnp.concatenate([kiwn[:, IDX_DIM + h:IDX_DIM + h + 1] for h in range(IDX_HEADS)], axis=0) * IDX_SCALE
    zpad = jnp.zeros((blk - ds, blk), F32)
    knew = jnp.concatenate([kvn[:, 0:blk], zpad], axis=0).T
    vnew = jnp.concatenate([kvn[:, blk:2 * blk], zpad], axis=0).T
    inew = jnp.concatenate([kiwn, zpad], axis=0).T[0:IDX_DIM, :]
    rowq = lax.broadcasted_iota(I32, (ds, blk), 0)
    lane = lax.broadcasted_iota(I32, (ds, blk), 1)
    allowed_new = lane <= rowq

    def iscore(kpage):
        r = jnp.maximum(jnp.dot(qis, kpage.astype(BF16), preferred_element_type=F32), 0.0) * wcol
        acc = r[0:ds]
        for h in range(1, IDX_HEADS):
            acc = acc + r[h * ds:(h + 1) * ds]
        return _sort_key(jnp.zeros((ds, blk), F32) + acc)

    for j in range(npages):
        ikey_ref[:, j * blk:(j + 1) * blk] = iscore(ipages[j][...])
    ikey_ref[:, npages * blk:nblk * blk] = jnp.where(allowed_new, iscore(inew), jnp.int32(INT_MIN))
    keys = ikey_ref[...]

    def count_ge(cand):
        return jnp.sum(jnp.where(keys >= cand, 1, 0), axis=1, keepdims=True)

    thr = jnp.full((ds, 1), INT_MIN, I32)
    for step in range(8):
        shift = 28 - 4 * step
        digit = jnp.zeros((ds, 1), I32)
        for c in range(1, 16):
            inc = int(np.array(c << shift, dtype=np.uint32).astype(np.int32))
            digit = digit + jnp.where(count_ge(thr + jnp.int32(inc)) >= n_sel, 1, 0)
        thr = thr + digit * jnp.int32(1 << shift)

    gt = keys > thr
    tie = keys == thr
    need = (n_sel - jnp.sum(jnp.where(gt, 1, 0), axis=1, keepdims=True)).astype(F32)
    tief = jnp.where(tie, 1.0, 0.0)
    ri = lax.broadcasted_iota(I32, (blk, blk), 0)
    ci = lax.broadcasted_iota(I32, (blk, blk), 1)
    utri = (ri <= ci).astype(F32)
    carry = jnp.zeros((ds, 1), F32)
    biases = []
    for j in range(nblk):
        sl = slice(j * blk, (j + 1) * blk)
        pre = jnp.dot(tief[:, sl], utri, preferred_element_type=F32) + carry
        sel = (tie[:, sl] & (pre <= need)) | gt[:, sl]
        if j == npages:
            sel = sel & allowed_new
        biases.append(jnp.where(sel, 0.0, MASK_NEG))
        carry = pre[:, blk - 1:blk]
    bias = jnp.concatenate(biases, axis=1)
    bias_g = jnp.concatenate([bias] * groups, axis=0)

    outs = [None] * C_HEADS
    for hk in range(C_KV_HEADS):
        qg = jnp.concatenate([q[:, (hk * groups + g) * hd:(hk * groups + g + 1) * hd] for g in range(groups)], axis=0).astype(BF16)
        ksl = slice(hk * hd, (hk + 1) * hd)
        ls = [jnp.dot(qg, kpages[j][ksl, :].astype(BF16), preferred_element_type=F32) for j in range(npages)]
        ls.append(jnp.dot(qg, knew[ksl, :].astype(BF16), preferred_element_type=F32))
        s = jnp.concatenate(ls, axis=1) + bias_g
        m = jnp.max(s, axis=1, keepdims=True)
        p = jnp.exp(s - m)
        l = jnp.sum(p, axis=1, keepdims=True)
        pb = p.astype(BF16)
        acc = _nt(pb[:, npages * blk:nblk * blk], vnew[ksl, :].astype(BF16))
        for j in range(npages):
            acc = acc + _nt(pb[:, j * blk:(j + 1) * blk], vpages[j][ksl, :].astype(BF16))
        og = acc / l
        for g in range(groups):
            outs[hk * groups + g] = og[g * ds:(g + 1) * ds, :]
    o = jnp.concatenate(outs, axis=1)
    o_ref[...] = o * (cz * _sigmoid(cz))


def _dsa_decode(u, cache_k, cache_v, cache_kidx, page_table, *, layer, nb, ds, n_sel):
    npages = page_table.shape[1]
    pt = page_table.reshape(-1)

    def page(j, width):
        return pl.BlockSpec((None, None, width, PAGE_SIZE), lambda b, pt_ref: (layer, pt_ref[b * npages + j], 0, 0))

    in_specs = [
        pl.BlockSpec((ds, 512), lambda b, pt_ref: (b, BLK_CQ)),
        pl.BlockSpec((ds, 256), lambda b, pt_ref: (b, BLK256_CQI)),
        pl.BlockSpec((ds, 512), lambda b, pt_ref: (b, BLK_CZ)),
        pl.BlockSpec((ds, 256), lambda b, pt_ref: (b, BLK256_KV)),
        pl.BlockSpec((ds, LANES), lambda b, pt_ref: (b, BLK_KIW)),
    ]
    args = [pt, u, u, u, u, u]
    for arr, width in ((cache_k, LANES), (cache_v, LANES), (cache_kidx, IDX_DIM)):
        in_specs += [page(j, width) for j in range(npages)]
        args += [arr] * npages
    grid_spec = pltpu.PrefetchScalarGridSpec(
        num_scalar_prefetch=1,
        grid=(nb,),
        in_specs=in_specs,
        out_specs=pl.BlockSpec((ds, C_WIDTH), lambda b, pt_ref: (b, 0)),
        scratch_shapes=[pltpu.VMEM((ds, (npages + 1) * LANES), I32)],
    )
    return pl.pallas_call(
        functools.partial(_dsa_decode_kernel, n_sel=n_sel, npages=npages, ds=ds),
        grid_spec=grid_spec,
        out_shape=jax.ShapeDtypeStruct((nb * ds, C_WIDTH), F32),
        compiler_params=_cparams(("arbitrary",)),
        name="dsa_decode",
    )(*args)


def _merge_kernel(h_ref, ya_ref, yb_ref, yc_ref, ga_ref, gb_ref, gc_ref, wpa_ref, wpb_ref, wpc_ref, wo_ref, fg_ref, o_ref, *, final):
    def proj(y_ref, w_ref):
        return jnp.dot(y_ref[...].astype(BF16), w_ref[...], preferred_element_type=F32)

    m = (_sigmoid(ga_ref[...]) * proj(ya_ref, wpa_ref)
         + _sigmoid(gb_ref[...]) * proj(yb_ref, wpb_ref)
         + _sigmoid(gc_ref[...]) * proj(yc_ref, wpc_ref))
    h = h_ref[...] + jnp.dot(m.astype(BF16), wo_ref[...], preferred_element_type=F32)
    if final:
        h = h * lax.rsqrt(jnp.mean(h * h, axis=-1, keepdims=True) + EPS) * fg_ref[...]
    o_ref[...] = h


def _merge(h, ya, yb, yc, u, wpa, wpb, wpc, wo, fg, *, final):
    rows = h.shape[0]
    tm = next(t for t in (512, 256, 128, 64, 32, 16, 8) if rows % t == 0)

    def rowblk(width, blk=0):
        return pl.BlockSpec((tm, width), lambda i: (i, blk))

    def full(a):
        return pl.BlockSpec(a.shape, lambda i: (0,) * a.ndim)

    return pl.pallas_call(
        functools.partial(_merge_kernel, final=final),
        grid=(rows // tm,),
        in_specs=[rowblk(D_MODEL), rowblk(512), rowblk(512), rowblk(512),
                  rowblk(D_MODEL, BLK_GATE), rowblk(D_MODEL, BLK_GATE + 1), rowblk(D_MODEL, BLK_GATE + 2),
                  full(wpa), full(wpb), full(wpc), full(wo), full(fg)],
        out_specs=rowblk(D_MODEL),
        out_shape=jax.ShapeDtypeStruct((rows, D_MODEL), F32),
        compiler_params=_cparams(("parallel",)),
        name="merge",
    )(h, ya, yb, yc, u, u, u, wpa, wpb, wpc, wo, fg)


def _rope_tables(pos):
    half = ROT_DIM // 2
    inv = ROPE_THETA ** (-jnp.arange(half, dtype=F32) * 2.0 / ROT_DIM)
    ang = pos.astype(F32)[:, None] * jnp.tile(inv, LANES // half)[None, :]
    return jnp.cos(ang), jnp.sin(ang)


def _cache_rows(u, nb, t_pad, t_valid):
    u3 = u.reshape(nb, t_pad, U_W)[:, :t_valid]
    k0, ki0 = BLK256_KV * 256, BLK_KIW * LANES
    heads = (nb, t_valid, C_KV_HEADS, C_HEAD_DIM)
    return (u3[..., k0:k0 + LANES].reshape(heads), u3[..., k0 + LANES:k0 + 2 * LANES].reshape(heads),
            u3[..., ki0:ki0 + IDX_DIM])


def _pack_cols(w):
    split = 4608
    tail = split + 68
    pad = jnp.zeros(w.shape[:-1] + (U_W - N_IN,), w.dtype)
    return jnp.concatenate([w[..., :split], w[..., tail:], w[..., split:tail], pad], axis=-1)


def kernel(x_prompt, x_sample, cache_k, cache_v, cache_kidx, state_hgrn, state_conv, page_table, meta_tokens, norm_g, w_in, b_in, lb_logits, hgrn_norm_g, conv_w, conv_b, conv_ln_g, conv_ln_b, conv_pw, w_pa, w_pb, w_pc, w_out, final_norm_g):
    nbp, seq, _ = x_prompt.shape
    nbs, ds, _ = x_sample.shape
    depth = w_in.shape[0]
    npages = page_table.shape[1]
    past = npages * PAGE_SIZE
    t_valid = seq + N_META
    t_pad = -(-t_valid // LANES) * LANES
    n_sel_p = min(TOPK_MAX, t_valid // 4)
    n_sel_s = min(TOPK_MAX, (past + ds) // 4)
    n_phys = cache_k.shape[1]

    meta = jnp.broadcast_to(meta_tokens[None].astype(F32), (nbp, N_META, D_MODEL))
    hp = jnp.concatenate([meta, x_prompt, jnp.zeros((nbp, t_pad - t_valid, D_MODEL), F32)], axis=1).reshape(nbp * t_pad, D_MODEL)
    hs = x_sample.reshape(nbs * ds, D_MODEL)
    cos_p, sin_p = _rope_tables(jnp.tile(jnp.arange(t_pad), nbp))
    cos_s, sin_s = _rope_tables(jnp.tile(past + jnp.arange(ds), nbs))

    w_in_p = _pack_cols(w_in).astype(BF16)
    b_in_p = _pack_cols(b_in).reshape(depth, 1, U_W)
    ck4 = cache_k.transpose(0, 1, 3, 4, 2).reshape(depth, n_phys, C_KV_HEADS * C_HEAD_DIM, PAGE_SIZE)
    cv4 = cache_v.transpose(0, 1, 3, 4, 2).reshape(depth, n_phys, C_KV_HEADS * C_HEAD_DIM, PAGE_SIZE)
    ci4 = cache_kidx.transpose(0, 1, 3, 2)
    fg = final_norm_g.reshape(1, D_MODEL)

    outs = {k: [] for k in ("pk", "pv", "pki", "ps", "pc", "sk", "sv", "ski", "ss", "sc")}
    for l in range(depth):
        g = norm_g[l].reshape(1, D_MODEL)
        gn = hgrn_norm_g[l].reshape(1, A_HEAD)
        cw, cb = conv_w[l], conv_b[l].reshape(1, B_WIDTH)
        lng, lnb = conv_ln_g[l].reshape(1, B_WIDTH), conv_ln_b[l].reshape(1, B_WIDTH)
        pw = conv_pw[l].astype(BF16)
        wpa, wpb, wpc, wo = (w[l].astype(BF16) for w in (w_pa, w_pb, w_pc, w_out))
        final = l == depth - 1

        u = _inproj(hp, g, w_in_p[l], b_in_p[l], cos_p, sin_p)
        ya, s_new = _hgrn_chunked(u, lb_logits, gn, layer=l, nb=nbp, t_pad=t_pad, t_valid=t_valid)
        yb, c_new = _conv_prompt(u, cw, cb, lng, lnb, pw, nb=nbp, t_pad=t_pad, t_valid=t_valid)
        yc = _dsa_prompt(u, nb=nbp, t_pad=t_pad, n_sel=n_sel_p)
        hp = _merge(hp, ya, yb, yc, u, wpa, wpb, wpc, wo, fg, final=final)
        k_new, v_new, ki_new = _cache_rows(u, nbp, t_pad, t_valid)
        for name, val in zip(("pk", "pv", "pki", "ps", "pc"), (k_new, v_new, ki_new, s_new, c_new)):
            outs[name].append(val)

        u = _inproj(hs, g, w_in_p[l], b_in_p[l], cos_s, sin_s)
        ya, s_new = _hgrn(u, lb_logits, gn, state_hgrn, layer=l, nb=nbs, t_pad=ds, t_valid=ds, rb=ds, sb=ds, state_layer=l)
        yb, c_new = _conv_decode(u, state_conv, cw, cb, lng, lnb, pw, layer=l, nb=nbs, ds=ds)
        yc = _dsa_decode(u, ck4, cv4, ci4, page_table, layer=l, nb=nbs, ds=ds, n_sel=n_sel_s)
        hs = _merge(hs, ya, yb, yc, u, wpa, wpb, wpc, wo, fg, final=final)
        k_new, v_new, ki_new = _cache_rows(u, nbs, ds, ds)
        for name, val in zip(("sk", "sv", "ski", "ss", "sc"), (k_new, v_new, ki_new, s_new, c_new)):
            outs[name].append(val)

    y_prompt = hp.reshape(nbp, t_pad, D_MODEL)[:, N_META:t_valid]
    y_sample = hs.reshape(nbs, ds, D_MODEL)
    st = {k: jnp.stack(v) for k, v in outs.items()}
    return (y_prompt, y_sample, st["pk"], st["pv"], st["pki"], st["ps"], st["pc"],
            st["sk"], st["sv"], st["ski"], st["ss"], st["sc"])
```

```python
import functools

import numpy as np
import jax
import jax.numpy as jnp
from jax import lax
from jax.experimental import pallas as pl
from jax.experimental.pallas import tpu as pltpu

F32 = jnp.float32
BF16 = jnp.bfloat16
I32 = jnp.int32
I16 = jnp.int16

D_MODEL = 1024
N_META = 16
EPS = 1e-6
MASK_NEG = -1e30
A_WIDTH = 512
A_HEAD = 128
A_HEADS = 4
B_WIDTH = 512
CONV_W = 31
C_HEADS = 8
C_HEAD_DIM = 64
C_WIDTH = 512
C_KV_HEADS = 2
IDX_HEADS = 4
IDX_DIM = 64
TOPK_MAX = 256
PAGE_SIZE = 128
ROPE_THETA = 500000.0
ROT_DIM = 16
IDX_SCALE = (IDX_HEADS * IDX_DIM) ** -0.5
INT_MIN = -(2 ** 31)

LANES = 128
SUBLANES = 8
VMEM_LIMIT = 48 * 1024 * 1024

N_IN = 8260
U_W = 8448
TN_IN = 768
BLK_AQ, BLK_AF, BLK_AI, BLK_AZ, BLK_GLUA, BLK_GLUG, BLK_BZ, BLK_CQ, BLK_KVQI, BLK_CZ = range(10)
BLK_GATE = 5
BLK_KIW = 64
BLK256_KV = 16
BLK256_CQI = 17


def _cparams(sem):
    return pltpu.CompilerParams(dimension_semantics=sem, vmem_limit_bytes=VMEM_LIMIT)


def _sigmoid(x):
    return 1.0 / (1.0 + jnp.exp(-x))


def _nt(a, b):
    return lax.dot_general(a, b, (((1,), (1,)), ((), ())), preferred_element_type=F32)


def _tn(a, b):
    return lax.dot_general(a, b, (((0,), (0,)), ((), ())), preferred_element_type=F32)


def _sort_key(x):
    bits = pltpu.bitcast(x, I32)
    return bits ^ ((bits >> 31) & jnp.int32(0x7FFFFFFF))


ROPE_TWO_HEADS = (28, 29, 30, 31, 32, 34, 35)
ROPE_ONE_HEAD = (BLK_KIW,)


def _inproj_kernel(x_ref, g_ref, w_ref, b_ref, cos_ref, sin_ref, o_ref, xn_ref):
    j = pl.program_id(1)

    @pl.when(j == 0)
    def _():
        x = x_ref[...]
        ms = jnp.mean(x * x, axis=-1, keepdims=True)
        xn_ref[...] = (x * lax.rsqrt(ms + EPS) * g_ref[...]).astype(BF16)

    o_ref[...] = jnp.dot(xn_ref[...], w_ref[...], preferred_element_type=F32) + b_ref[...]

    per_tile = TN_IN // LANES
    half = ROT_DIM // 2
    lane = lax.broadcasted_iota(I32, (1, LANES), 1)
    for tile in sorted({blk // per_tile for blk in ROPE_TWO_HEADS + ROPE_ONE_HEAD}):
        @pl.when(j == tile)
        def _(tile=tile):
            cosd = cos_ref[...]
            sind = sin_ref[...]
            for blk in ROPE_TWO_HEADS + ROPE_ONE_HEAD:
                if blk // per_tile != tile:
                    continue
                pos = lane % C_HEAD_DIM if blk in ROPE_TWO_HEADS else lane
                c = jnp.where(pos < ROT_DIM, cosd, 1.0)
                a = jnp.where((pos >= half) & (pos < ROT_DIM), sind, 0.0)
                b = jnp.where(pos < half, -sind, 0.0)
                sl = slice((blk % per_tile) * LANES, (blk % per_tile + 1) * LANES)
                x = o_ref[:, sl]
                o_ref[:, sl] = x * c + pltpu.roll(x, half, 1) * a + pltpu.roll(x, LANES - half, 1) * b


def _inproj(h, g, w, b, cosd, sind):
    rows = h.shape[0]
    tm = next(t for t in (1024, 512, 256, 128, 64, 32, 16, 8) if rows % t == 0)
    return pl.pallas_call(
        _inproj_kernel,
        grid=(rows // tm, U_W // TN_IN),
        in_specs=[
            pl.BlockSpec((tm, D_MODEL), lambda i, j: (i, 0)),
            pl.BlockSpec((1, D_MODEL), lambda i, j: (0, 0)),
            pl.BlockSpec((D_MODEL, TN_IN), lambda i, j: (0, j)),
            pl.BlockSpec((1, TN_IN), lambda i, j: (0, j)),
            pl.BlockSpec((tm, LANES), lambda i, j: (i, 0)),
            pl.BlockSpec((tm, LANES), lambda i, j: (i, 0)),
        ],
        out_specs=pl.BlockSpec((tm, TN_IN), lambda i, j: (i, j)),
        out_shape=jax.ShapeDtypeStruct((rows, U_W), F32),
        scratch_shapes=[pltpu.VMEM((tm, D_MODEL), BF16)],
        compiler_params=_cparams(("parallel", "arbitrary")),
        name="inproj",
    )(h, g, w, b, cosd, sind)


def _hgrn_kernel(*refs, layer, sb, rb, t_valid, has_s0):
    aq_ref, af_ref, ai_ref, az_ref, lbl_ref, gn_ref = refs[:6]
    if has_s0:
        s0_ref, o_ref, sout_ref, st_ref = refs[6:]
    else:
        o_ref, sout_ref, st_ref = refs[6:]
    j = pl.program_id(1)
    nsub_max = rb // sb

    @pl.when(j == 0)
    def _():
        for h in range(A_HEADS):
            if has_s0:
                st_ref[h] = s0_ref[h].T
            else:
                st_ref[h] = jnp.zeros((A_HEAD, A_HEAD), F32)

    lg = lbl_ref[...]
    e = jnp.exp(lg - jnp.max(lg, axis=0, keepdims=True))
    sm = e / jnp.sum(e, axis=0, keepdims=True)
    lb = jnp.zeros((1, A_WIDTH), F32)
    for i in range(1, layer + 1):
        lb = lb + sm[i:i + 1, :]

    n_sub = jnp.clip((t_valid - j * rb) // sb, 0, nsub_max)
    if nsub_max > 1:
        o_ref[...] = jnp.zeros((rb, A_WIDTH), F32)
    ri = lax.broadcasted_iota(I32, (sb, sb), 0)
    ci = lax.broadcasted_iota(I32, (sb, sb), 1)
    tri = (ri >= ci).astype(F32)
    rowi = lax.broadcasted_iota(I32, (sb, 1), 0)
    gn = gn_ref[...]

    def sub(i, carry):
        r0 = pl.multiple_of(i * sb, sb)
        f = af_ref[pl.ds(r0, sb), :]
        aq = aq_ref[pl.ds(r0, sb), :]
        v = ai_ref[pl.ds(r0, sb), :]
        az = az_ref[pl.ds(r0, sb), :]
        logf = jnp.log(lb + (1.0 - lb) * _sigmoid(f))
        kk = (1.0 - lb) * _sigmoid(-f)
        q = aq * _sigmoid(aq)
        b = jnp.dot(tri, logf, preferred_element_type=F32, precision=lax.Precision.HIGHEST)
        blast = b[sb - 1:sb, :]
        qe = q * jnp.exp(b)
        kd = kk * jnp.exp(blast - b)
        dec = jnp.exp(blast)
        outs = []
        for h in range(A_HEADS):
            sl = slice(h * A_HEAD, (h + 1) * A_HEAD)
            st = st_ref[h]
            o = _nt(qe[:, sl].astype(BF16), st.astype(BF16))
            qh, kh, vh, bh = q[:, sl], kk[:, sl], v[:, sl], b[:, sl]
            for s in range(sb):
                w = jnp.exp(bh - bh[s:s + 1, :])
                a = jnp.sum(qh * (kh[s:s + 1, :] * w), axis=-1, keepdims=True)
                o = o + jnp.where(rowi >= s, a, 0.0) * vh[s:s + 1, :]
            st_ref[h] = st * dec[:, sl] + _tn(vh.astype(BF16), kd[:, sl].astype(BF16))
            on = o * lax.rsqrt(jnp.mean(o * o, axis=-1, keepdims=True) + EPS) * gn
            outs.append(on)
        o_ref[pl.ds(r0, sb), :] = jnp.concatenate(outs, axis=1) * (az * _sigmoid(az))
        return carry

    lax.fori_loop(0, n_sub, sub, 0)

    @pl.when(j == pl.num_programs(1) - 1)
    def _():
        for h in range(A_HEADS):
            sout_ref[h] = st_ref[h].T


def _hgrn(u, lb_logits, gn_g, s0, *, layer, nb, t_pad, t_valid, rb, sb, state_layer=None):
    nj = t_pad // rb
    has_s0 = s0 is not None

    def ublk(blk):
        return pl.BlockSpec((rb, 512), lambda b, j: (b * nj + j, blk))

    in_specs = [ublk(BLK_AQ), ublk(BLK_AF), ublk(BLK_AI), ublk(BLK_AZ),
                pl.BlockSpec(lb_logits.shape, lambda b, j: (0, 0)),
                pl.BlockSpec((1, A_HEAD), lambda b, j: (0, 0))]
    args = [u, u, u, u, lb_logits, gn_g]
    if has_s0:
        in_specs.append(pl.BlockSpec((None, None, A_HEADS, A_HEAD, A_HEAD), lambda b, j: (state_layer, b, 0, 0, 0)))
        args.append(s0)
    return pl.pallas_call(
        functools.partial(_hgrn_kernel, layer=layer, sb=sb, rb=rb, t_valid=t_valid, has_s0=has_s0),
        grid=(nb, nj),
        in_specs=in_specs,
        out_specs=[pl.BlockSpec((rb, A_WIDTH), lambda b, j: (b * nj + j, 0)),
                   pl.BlockSpec((None, A_HEADS, A_HEAD, A_HEAD), lambda b, j: (b, 0, 0, 0))],
        out_shape=[jax.ShapeDtypeStruct((nb * t_pad, A_WIDTH), F32),
                   jax.ShapeDtypeStruct((nb, A_HEADS, A_HEAD, A_HEAD), F32)],
        scratch_shapes=[pltpu.VMEM((A_HEADS, A_HEAD, A_HEAD), F32)],
        compiler_params=_cparams(("parallel", "arbitrary")),
        name="hgrn",
    )(*args)


def _hgrn_levels(c):
    out, size = [], SUBLANES
    while size < c:
        out.append(size)
        size *= 2
    return out


def _hgrn_chunk_kernel(aq_ref, af_ref, ai_ref, az_ref, lbl_ref, gn_ref, o_ref, sout_ref,
                       st_ref, q_s, k_s, b_s, oi_s, dg_s, qf_s, kf_s, mask_s, *, layer, c, t_valid):
    j = pl.program_id(1)
    levels = _hgrn_levels(c)
    ri = lax.broadcasted_iota(I32, (c, c), 0)
    ci = lax.broadcasted_iota(I32, (c, c), 1)

    @pl.when(j == 0)
    def _():
        for h in range(A_HEADS):
            st_ref[h] = jnp.zeros((A_HEAD, A_HEAD), F32)
        for n, size in enumerate(levels):
            same = (ri // (2 * size)) == (ci // (2 * size))
            mask_s[n] = jnp.where(same & ((ri % (2 * size)) >= size) & ((ci % (2 * size)) < size), 1.0, 0.0)

    lg = lbl_ref[...]
    e = jnp.exp(lg - jnp.max(lg, axis=0, keepdims=True))
    sm = e / jnp.sum(e, axis=0, keepdims=True)
    lb = jnp.zeros((1, A_WIDTH), F32)
    for i in range(1, layer + 1):
        lb = lb + sm[i:i + 1, :]

    valid = (j * c + lax.broadcasted_iota(I32, (c, 1), 0)) < t_valid
    f = af_ref[...]
    aq = aq_ref[...]
    logf = jnp.where(valid, jnp.log(lb + (1.0 - lb) * _sigmoid(f)), 0.0)
    kk = jnp.where(valid, (1.0 - lb) * _sigmoid(-f), 0.0)
    q = aq * _sigmoid(aq)
    b = jnp.dot((ri >= ci).astype(F32), logf, preferred_element_type=F32, precision=lax.Precision.HIGHEST)
    blast = b[c - 1:c, :]
    q_s[...] = q
    k_s[...] = kk
    b_s[...] = b
    for n, size in enumerate(levels):
        groups = c // (2 * size)
        bref = jnp.broadcast_to(b.reshape(groups, 2 * size, A_WIDTH)[:, size - 1:size, :], (groups, 2 * size, A_WIDTH))
        fac = jnp.exp(-jnp.abs(b - bref.reshape(c, A_WIDTH)))
        qf_s[n] = (q * fac).astype(BF16)
        kf_s[n] = (kk * fac).astype(BF16)
    qe = (q * jnp.exp(b)).astype(BF16)
    kd = (kk * jnp.exp(blast - b)).astype(BF16)
    dec = jnp.exp(blast)
    vb = ai_ref[...].astype(BF16)
    for h in range(A_HEADS):
        sl = slice(h * A_HEAD, (h + 1) * A_HEAD)
        st = st_ref[h]
        o = _nt(qe[:, sl], st.astype(BF16))
        if levels:
            att = jnp.zeros((c, c), F32)
            for n in range(len(levels)):
                att = att + _nt(qf_s[n, :, sl], kf_s[n, :, sl]) * mask_s[n]
            o = o + jnp.dot(att.astype(BF16), vb[:, sl], preferred_element_type=F32)
        oi_s[:, sl] = o
        st_ref[h] = st * dec[:, sl] + _tn(vb[:, sl], kd[:, sl])

    row8 = lax.broadcasted_iota(I32, (SUBLANES, 1), 0)
    for x in range(c // SUBLANES):
        rows = slice(x * SUBLANES, (x + 1) * SUBLANES)
        outs = []
        for h in range(A_HEADS):
            sl = slice(h * A_HEAD, (h + 1) * A_HEAD)
            qh, kh, bh, vh = q_s[rows, sl], k_s[rows, sl], b_s[rows, sl], ai_ref[rows, sl]
            o = jnp.zeros((SUBLANES, A_HEAD), F32)
            for s in range(SUBLANES):
                w = jnp.exp(bh - bh[s:s + 1, :])
                a = jnp.sum(qh * (kh[s:s + 1, :] * w), axis=-1, keepdims=True)
                o = o + jnp.where(row8 >= s, a, 0.0) * vh[s:s + 1, :]
            outs.append(o)
        dg_s[rows, :] = jnp.concatenate(outs, axis=1)

    gn = gn_ref[...]
    az = az_ref[...]
    outs = []
    for h in range(A_HEADS):
        sl = slice(h * A_HEAD, (h + 1) * A_HEAD)
        o = oi_s[:, sl] + dg_s[:, sl]
        outs.append(o * lax.rsqrt(jnp.mean(o * o, axis=-1, keepdims=True) + EPS) * gn)
    o_ref[...] = jnp.concatenate(outs, axis=1) * (az * _sigmoid(az))

    @pl.when(j == pl.num_programs(1) - 1)
    def _():
        for h in range(A_HEADS):
            sout_ref[h] = st_ref[h].T


def _hgrn_chunked(u, lb_logits, gn_g, *, layer, nb, t_pad, t_valid):
    c = LANES
    nj = t_pad // c
    nlev = max(len(_hgrn_levels(c)), 1)

    def ublk(blk):
        return pl.BlockSpec((c, 512), lambda b, j: (b * nj + j, blk))

    return pl.pallas_call(
        functools.partial(_hgrn_chunk_kernel, layer=layer, c=c, t_valid=t_valid),
        grid=(nb, nj),
        in_specs=[ublk(BLK_AQ), ublk(BLK_AF), ublk(BLK_AI), ublk(BLK_AZ),
                  pl.BlockSpec(lb_logits.shape, lambda b, j: (0, 0)),
                  pl.BlockSpec((1, A_HEAD), lambda b, j: (0, 0))],
        out_specs=[pl.BlockSpec((c, A_WIDTH), lambda b, j: (b * nj + j, 0)),
                   pl.BlockSpec((None, A_HEADS, A_HEAD, A_HEAD), lambda b, j: (b, 0, 0, 0))],
        out_shape=[jax.ShapeDtypeStruct((nb * t_pad, A_WIDTH), F32),
                   jax.ShapeDtypeStruct((nb, A_HEADS, A_HEAD, A_HEAD), F32)],
        scratch_shapes=[pltpu.VMEM((A_HEADS, A_HEAD, A_HEAD), F32),
                        pltpu.VMEM((c, A_WIDTH), F32),
                        pltpu.VMEM((c, A_WIDTH), F32),
                        pltpu.VMEM((c, A_WIDTH), F32),
                        pltpu.VMEM((c, A_WIDTH), F32),
                        pltpu.VMEM((c, A_WIDTH), F32),
                        pltpu.VMEM((nlev, c, A_WIDTH), BF16),
                        pltpu.VMEM((nlev, c, A_WIDTH), BF16),
                        pltpu.VMEM((nlev, c, c), F32)],
        compiler_params=_cparams(("parallel", "arbitrary")),
        name="hgrn_chunked",
    )(u, u, u, u, lb_logits, gn_g)


def _conv_tail(y, bz, lng, lnb, pw_ref):
    mu = jnp.mean(y, axis=-1, keepdims=True)
    d = y - mu
    var = jnp.mean(d * d, axis=-1, keepdims=True)
    y = d * lax.rsqrt(var + EPS) * lng + lnb
    y = y * _sigmoid(y)
    z = jnp.dot(y.astype(BF16), pw_ref[...], preferred_element_type=F32)
    return z * (bz * _sigmoid(bz))


def _conv_prompt_kernel(a_ref, g_ref, bz_ref, w_ref, cb_ref, lng_ref, lnb_ref, pw_ref, o_ref, st_ref, xb_ref, y_ref, xs_ref, *, rb, tail_off):
    j = pl.program_id(1)
    halo = 32

    @pl.when(j == 0)
    def _():
        xb_ref[0:halo, :] = jnp.zeros((halo, B_WIDTH), F32)

    xb_ref[halo:halo + rb, :] = a_ref[...] * _sigmoid(g_ref[...])
    span = rb + halo - SUBLANES
    for s in range(1, SUBLANES):
        xs_ref[s - 1, 0:span, :] = xb_ref[pl.ds(s, span), :]
    rc = 64
    for c in range(B_WIDTH // LANES):
        cs = slice(c * LANES, (c + 1) * LANES)
        wc = w_ref[:, cs]
        for r in range(rb // rc):
            acc = jnp.zeros((rc, LANES), F32) + cb_ref[:, cs]
            for t in range(CONV_W):
                off = halo - (CONV_W - 1) + t
                s, base = off % SUBLANES, (off // SUBLANES) * SUBLANES + r * rc
                win = xb_ref[base:base + rc, cs] if s == 0 else xs_ref[s - 1, base:base + rc, cs]
                acc = acc + win * wc[t:t + 1, :]
            y_ref[r * rc:(r + 1) * rc, cs] = acc
    o_ref[...] = _conv_tail(y_ref[...], bz_ref[...], lng_ref[...], lnb_ref[...], pw_ref)

    @pl.when(j == pl.num_programs(1) - 1)
    def _():
        st_ref[...] = xb_ref[pl.ds(tail_off, CONV_W - 1), :]

    xb_ref[0:halo, :] = xb_ref[rb:rb + halo, :]


def _conv_prompt(u, cw, cb, lng, lnb, pw, *, nb, t_pad, t_valid):
    rb = 128
    nj = t_pad // rb
    tail_off = (t_valid - (CONV_W - 1)) - ((nj - 1) * rb - 32)
    assert 0 <= tail_off and tail_off + CONV_W - 1 <= rb + 32

    def ublk(blk):
        return pl.BlockSpec((rb, 512), lambda b, j: (b * nj + j, blk))

    def full(a):
        return pl.BlockSpec(a.shape, lambda b, j: (0,) * a.ndim)

    return pl.pallas_call(
        functools.partial(_conv_prompt_kernel, rb=rb, tail_off=tail_off),
        grid=(nb, nj),
        in_specs=[ublk(BLK_GLUA), ublk(BLK_GLUG), ublk(BLK_BZ), full(cw), full(cb), full(lng), full(lnb), full(pw)],
        out_specs=[pl.BlockSpec((rb, B_WIDTH), lambda b, j: (b * nj + j, 0)),
                   pl.BlockSpec((None, CONV_W - 1, B_WIDTH), lambda b, j: (b, 0, 0))],
        out_shape=[jax.ShapeDtypeStruct((nb * t_pad, B_WIDTH), F32),
                   jax.ShapeDtypeStruct((nb, CONV_W - 1, B_WIDTH), F32)],
        scratch_shapes=[pltpu.VMEM((rb + 32, B_WIDTH), F32), pltpu.VMEM((rb, B_WIDTH), F32),
                        pltpu.VMEM((SUBLANES - 1, rb + 32, B_WIDTH), F32)],
        compiler_params=_cparams(("parallel", "arbitrary")),
        name="conv_prompt",
    )(u, u, u, cw, cb, lng, lnb, pw)


def _conv_decode_kernel(a_ref, g_ref, bz_ref, sin_ref, w_ref, cb_ref, lng_ref, lnb_ref, pw_ref, o_ref, sout_ref, xs_ref, *, ns, ds):
    nbuf = CONV_W - 1
    xs_ref[:, 0:nbuf, :] = sin_ref[...]
    glu = a_ref[...] * _sigmoid(g_ref[...])
    xs_ref[:, nbuf:nbuf + ds, :] = glu.reshape(ns, ds, B_WIDTH)
    acc = jnp.zeros((ns, ds, B_WIDTH), F32) + cb_ref[...]
    for t in range(CONV_W):
        acc = acc + xs_ref[:, pl.ds(t, ds), :] * w_ref[t:t + 1, :]
    o_ref[...] = _conv_tail(acc.reshape(ns * ds, B_WIDTH), bz_ref[...], lng_ref[...], lnb_ref[...], pw_ref)
    sout_ref[...] = xs_ref[:, pl.ds(ds, nbuf), :]


def _conv_decode(u, state, cw, cb, lng, lnb, pw, *, layer, nb, ds):
    ns = next(t for t in (16, 8, 4, 2, 1) if nb % t == 0)
    rb = ns * ds

    def ublk(blk):
        return pl.BlockSpec((rb, 512), lambda i: (i, blk))

    def full(a):
        return pl.BlockSpec(a.shape, lambda i: (0,) * a.ndim)

    return pl.pallas_call(
        functools.partial(_conv_decode_kernel, ns=ns, ds=ds),
        grid=(nb // ns,),
        in_specs=[ublk(BLK_GLUA), ublk(BLK_GLUG), ublk(BLK_BZ),
                  pl.BlockSpec((None, ns, CONV_W - 1, B_WIDTH), lambda i: (layer, i, 0, 0)),
                  full(cw), full(cb), full(lng), full(lnb), full(pw)],
        out_specs=[pl.BlockSpec((rb, B_WIDTH), lambda i: (i, 0)),
                   pl.BlockSpec((ns, CONV_W - 1, B_WIDTH), lambda i: (i, 0, 0))],
        out_shape=[jax.ShapeDtypeStruct((nb * ds, B_WIDTH), F32),
                   jax.ShapeDtypeStruct((nb, CONV_W - 1, B_WIDTH), F32)],
        scratch_shapes=[pltpu.VMEM((ns, CONV_W - 1 + ds, B_WIDTH), F32)],
        compiler_params=_cparams(("parallel",)),
        name="conv_decode",
    )(u, u, u, state, cw, cb, lng, lnb, pw)


def _dsa_tile_kernel(q_ref, qi_ref, cz_ref, kv_ref, kiw_ref, o_ref,
                     k16, vt16, ki16, wq_s, ik, hi16, lo16, s_buf, acc_s, *, n_sel, nblk, nb1, qw, t_pad):
    i = pl.program_id(1)
    blk = LANES
    hd = C_HEAD_DIM
    vregs = blk // SUBLANES
    qblocks = qw // blk
    groups = C_HEADS // C_KV_HEADS
    gw = groups * qw

    @pl.when(i == 0)
    def _():
        for kb in range(nblk):
            sl = slice(kb * blk, (kb + 1) * blk)
            x = kv_ref[sl, :]
            for hk in range(C_KV_HEADS):
                k16[hk * nb1 + kb] = x[:, hk * hd:(hk + 1) * hd].astype(BF16)
            vt16[kb] = x[:, blk:2 * blk].T.astype(BF16)
            ki16[kb] = kiw_ref[sl, :][:, 0:IDX_DIM].astype(BF16)
        for kb in range(nblk, nb1):
            for hk in range(C_KV_HEADS):
                k16[hk * nb1 + kb] = jnp.zeros((blk, hd), BF16)
            vt16[kb] = jnp.zeros((blk, blk), BF16)
            ki16[kb] = jnp.zeros((blk, IDX_DIM), BF16)
        wq_s[0:t_pad, :] = kiw_ref[...]
        if nb1 * blk > t_pad:
            wq_s[t_pad:nb1 * blk, :] = jnp.zeros((nb1 * blk - t_pad, blk), F32)

    npair = ((i + 1) * qblocks + 1) // 2
    r0 = pl.multiple_of(i * qw, qw)
    wq = wq_s[pl.ds(r0, qw), :]
    w_t = jnp.concatenate([wq[a * blk:(a + 1) * blk, :].T for a in range(qblocks)], axis=1)
    qi = qi_ref[...].astype(BF16)
    rowi = lax.broadcasted_iota(I32, (blk, qw), 0)
    qpos = r0 + lax.broadcasted_iota(I32, (blk, qw), 1)

    qis = jnp.concatenate([qi[:, h * IDX_DIM:(h + 1) * IDX_DIM] for h in range(IDX_HEADS)], axis=0)
    w_h = [w_t[IDX_DIM + h:IDX_DIM + h + 1, :] * IDX_SCALE for h in range(IDX_HEADS)]

    def p1(pp, c):
        for u in range(2):
            kb = 2 * pp + u
            s = _nt(ki16[kb], qis)
            acc = jnp.zeros((blk, qw), F32)
            for h in range(IDX_HEADS):
                acc = acc + jnp.maximum(s[:, h * qw:(h + 1) * qw], 0.0) * w_h[h]
            allowed = (kb * blk + rowi) <= qpos
            key = jnp.where(allowed, _sort_key(acc), jnp.int32(INT_MIN))
            ik[kb] = key
            hi16[kb] = (key >> 16).astype(I16)
        return c

    lax.fori_loop(0, npair, p1, 0)

    one16 = jnp.ones((), BF16)
    zero16 = jnp.zeros((), BF16)

    def count16(ref, pred):
        def body(pp, c):
            parts = []
            for u in range(2):
                m = jnp.where(pred(ref[2 * pp + u]), one16, zero16).reshape(vregs // 2, 2 * SUBLANES, qw)
                parts += [m[j] for j in range(vregs // 2)]
            while len(parts) > 1:
                parts = [parts[j] + parts[j + 1] for j in range(0, len(parts), 2)]
            return c + parts[0]
        c = lax.fori_loop(0, npair, body, jnp.zeros((2 * SUBLANES, qw), BF16))
        return jnp.sum(c.astype(F32), axis=0, keepdims=True)

    def search16(ref, want):
        lo = jnp.full((1, qw), -(1 << 15), I32)
        zero = jnp.zeros((1, qw), I32)
        v = jnp.where(count16(ref, lambda k: k >= zero.astype(I16)) >= want, zero, lo)

        def bit_body(t, v):
            cand = v + lax.shift_left(jnp.int32(1), jnp.int32(14) - t)
            return jnp.where(count16(ref, lambda k: k >= cand.astype(I16)) >= want, cand, v)

        return lax.fori_loop(0, 15, bit_body, v)

    thr_hi = search16(hi16, jnp.float32(n_sel))
    thr_hi16 = thr_hi.astype(I16)
    want_lo = n_sel - count16(hi16, lambda k: k > thr_hi16)

    def p2(pp, c):
        for u in range(2):
            kb = 2 * pp + u
            low = (ik[kb] & jnp.int32(0xFFFF)) - jnp.int32(1 << 15)
            lo16[kb] = jnp.where((ik[kb] >> 16) == thr_hi, low, jnp.int32(-(1 << 15))).astype(I16)
        return c

    lax.fori_loop(0, npair, p2, 0)
    thr_lo = search16(lo16, want_lo)
    thr = thr_hi * jnp.int32(1 << 16) + (thr_lo + jnp.int32(1 << 15))

    def count32(pred):
        def body(pp, c):
            for u in range(2):
                m = jnp.where(pred(ik[2 * pp + u]), 1, 0)
                c = c + jnp.sum(m.reshape(vregs, SUBLANES, qw), axis=0)
            return c
        c = lax.fori_loop(0, npair, body, jnp.zeros((SUBLANES, qw), I32))
        return jnp.sum(c, axis=0, keepdims=True)

    need = (n_sel - count32(lambda k: k > thr)).astype(F32)

    ltri = (lax.broadcasted_iota(I32, (blk, blk), 0) >= lax.broadcasted_iota(I32, (blk, blk), 1)).astype(BF16)
    q = (q_ref[...] * C_HEAD_DIM ** -0.5).astype(BF16)
    qg = [jnp.concatenate([q[:, (hk * groups + g) * hd:(hk * groups + g + 1) * hd] for g in range(groups)], axis=0)
          for hk in range(C_KV_HEADS)]

    def p3(pp, carry):
        tcar = carry[0]
        mx = list(carry[1:])
        for u in range(2):
            kb = 2 * pp + u
            key = ik[kb]
            tie = key == thr
            pre = jnp.dot(ltri, jnp.where(tie, 1.0, 0.0).astype(BF16), preferred_element_type=F32) + tcar
            allowed = (kb * blk + rowi) <= qpos
            sel = ((tie & (pre <= need)) | (key > thr)) & allowed
            bias = jnp.where(sel, 0.0, MASK_NEG)
            tcar = pre[blk - 1:blk, :]
            bias_g = jnp.concatenate([bias] * groups, axis=1)
            for hk in range(C_KV_HEADS):
                s = _nt(k16[hk * nb1 + kb], qg[hk]) + bias_g
                s_buf[hk * nb1 + kb] = s
                mx[hk] = jnp.maximum(mx[hk], jnp.max(s.reshape(vregs, SUBLANES, gw), axis=0))
        return (tcar, *mx)

    init = (jnp.zeros((1, qw), F32),) + tuple(jnp.full((SUBLANES, gw), -3.0e38, F32) for _ in range(C_KV_HEADS))
    res = lax.fori_loop(0, npair, p3, init)
    m = [jnp.max(res[1 + hk], axis=0, keepdims=True) for hk in range(C_KV_HEADS)]

    acc_s[...] = jnp.zeros((C_KV_HEADS, hd, gw), F32)

    def p4(pp, carry):
        ls = list(carry)
        for u in range(2):
            kb = 2 * pp + u
            vt = vt16[kb]
            for hk in range(C_KV_HEADS):
                p = jnp.exp(s_buf[hk * nb1 + kb] - m[hk])
                ls[hk] = ls[hk] + jnp.sum(p.reshape(vregs, SUBLANES, gw), axis=0)
                acc_s[hk] = acc_s[hk] + jnp.dot(vt[hk * hd:(hk + 1) * hd, :], p.astype(BF16), preferred_element_type=F32)
        return tuple(ls)

    ls = lax.fori_loop(0, npair, p4, tuple(jnp.zeros((SUBLANES, gw), F32) for _ in range(C_KV_HEADS)))
    outs = []
    for hk in range(C_KV_HEADS):
        og = acc_s[hk] / jnp.sum(ls[hk], axis=0, keepdims=True)
        outs += [og[:, g * qw:(g + 1) * qw] for g in range(groups)]
    o_t = jnp.concatenate(outs, axis=0)
    o = jnp.concatenate(
        [jnp.concatenate([o_t[j * blk:(j + 1) * blk, a * blk:(a + 1) * blk].T for j in range(C_WIDTH // blk)], axis=1)
         for a in range(qblocks)], axis=0)
    cz = cz_ref[...]
    o_ref[...] = o * (cz * _sigmoid(cz))


def _dsa_tiles(u, *, nb, t_pad, n_sel):
    blk = LANES
    qw = 2 * blk
    nblk = t_pad // blk
    nq = -(-t_pad // qw)
    nb1 = nq * (qw // blk)
    gw = (C_HEADS // C_KV_HEADS) * qw
    u3 = u.reshape(nb, t_pad, U_W)
    out = pl.pallas_call(
        functools.partial(_dsa_tile_kernel, n_sel=n_sel, nblk=nblk, nb1=nb1, qw=qw, t_pad=t_pad),
        grid=(nb, nq),
        in_specs=[
            pl.BlockSpec((None, qw, 512), lambda b, i: (b, i, BLK_CQ)),
            pl.BlockSpec((None, qw, 256), lambda b, i: (b, i, BLK256_CQI)),
            pl.BlockSpec((None, qw, 512), lambda b, i: (b, i, BLK_CZ)),
            pl.BlockSpec((None, t_pad, 256), lambda b, i: (b, 0, BLK256_KV)),
            pl.BlockSpec((None, t_pad, LANES), lambda b, i: (b, 0, BLK_KIW)),
        ],
        out_specs=pl.BlockSpec((None, qw, C_WIDTH), lambda b, i: (b, i, 0)),
        out_shape=jax.ShapeDtypeStruct((nb, t_pad, C_WIDTH), F32),
        scratch_shapes=[
            pltpu.VMEM((C_KV_HEADS * nb1, blk, C_HEAD_DIM), BF16),
            pltpu.VMEM((nb1, blk, blk), BF16),
            pltpu.VMEM((nb1, blk, IDX_DIM), BF16),
            pltpu.VMEM((nb1 * blk, LANES), F32),
            pltpu.VMEM((nb1, blk, qw), I32),
            pltpu.VMEM((nb1, blk, qw), I16),
            pltpu.VMEM((nb1, blk, qw), I16),
            pltpu.VMEM((C_KV_HEADS * nb1, blk, gw), F32),
            pltpu.VMEM((C_KV_HEADS, C_HEAD_DIM, gw), F32),
        ],
        compiler_params=_cparams(("parallel", "arbitrary")),
        name="dsa_tiles",
    )(u3, u3, u3, u3, u3)
    return out.reshape(nb * t_pad, C_WIDTH)


def _dsa_decode_kernel(pt_ref, q_ref, qi_ref, cz_ref, kvn_ref, kiwn_ref, *rest, n_sel, npages, ds):
    del pt_ref
    kpages = rest[0:npages]
    vpages = rest[npages:2 * npages]
    ipages = rest[2 * npages:3 * npages]
    o_ref, kcat, vcat, icat = rest[3 * npages:3 * npages + 4]
    q, qi, cz, kvn, kiwn = q_ref[...] * C_HEAD_DIM ** -0.5, qi_ref[...], cz_ref[...], kvn_ref[...], kiwn_ref[...]
    blk = LANES
    hd = C_HEAD_DIM
    nblk = npages + 1
    nkeys = nblk * blk
    groups = C_HEADS // C_KV_HEADS

    zpad = jnp.zeros((blk - ds, blk), F32)
    for j in range(npages):
        sl = slice(j * blk, (j + 1) * blk)
        kcat[:, sl] = kpages[j][...].astype(BF16)
        vcat[:, sl] = vpages[j][...].astype(BF16)
        icat[:, sl] = ipages[j][...].astype(BF16)
    new = slice(npages * blk, nkeys)
    kcat[:, new] = jnp.concatenate([kvn[:, 0:blk], zpad], axis=0).T.astype(BF16)
    vcat[:, new] = jnp.concatenate([kvn[:, blk:2 * blk], zpad], axis=0).T.astype(BF16)
    icat[:, new] = jnp.concatenate([kiwn, zpad], axis=0).T[0:IDX_DIM, :].astype(BF16)
    rowq = lax.broadcasted_iota(I32, (ds, nkeys), 0)
    kpos = lax.broadcasted_iota(I32, (ds, nkeys), 1)
    allowed = kpos <= npages * blk + rowq

    qis = jnp.concatenate([qi[:, h * IDX_DIM:(h + 1) * IDX_DIM] for h in range(IDX_HEADS)], axis=0).astype(BF16)
    wcol = jnp.concatenate([kiwn[:, IDX_DIM + h:IDX_DIM + h + 1] for h in range(IDX_HEADS)], axis=0) * IDX_SCALE
    r = jnp.maximum(jnp.dot(qis, icat[...], preferred_element_type=F32), 0.0) * wcol
    acc = r[0:ds]
    for h in range(1, IDX_HEADS):
        acc = acc + r[h * ds:(h + 1) * ds]
    keys = jnp.where(allowed, _sort_key(jnp.zeros((ds, nkeys), F32) + acc), jnp.int32(INT_MIN))

    def count_ge(cand):
        return jnp.sum(jnp.where(keys >= cand, 1, 0), axis=1, keepdims=True)

    thr = jnp.full((ds, 1), INT_MIN, I32)
    for step in range(8):
        shift = 28 - 4 * step
        digit = jnp.zeros((ds, 1), I32)
        for c in range(1, 16):
            inc = int(np.array(c << shift, dtype=np.uint32).astype(np.int32))
            digit = digit + jnp.where(count_ge(thr + jnp.int32(inc)) >= n_sel, 1, 0)
        thr = thr + digit * jnp.int32(1 << shift)

    gt = keys > thr
    tie = keys == thr
    need = (n_sel - jnp.sum(jnp.where(gt, 1, 0), axis=1, keepdims=True)).astype(F32)
    tief = jnp.where(tie, 1.0, 0.0)
    ri = lax.broadcasted_iota(I32, (blk, blk), 0)
    ci = lax.broadcasted_iota(I32, (blk, blk), 1)
    utri = (ri <= ci).astype(F32)
    carry = jnp.zeros((ds, 1), F32)
    biases = []
    for j in range(nblk):
        sl = slice(j * blk, (j + 1) * blk)
        pre = jnp.dot(tief[:, sl], utri, preferred_element_type=F32) + carry
        sel = ((tie[:, sl] & (pre <= need)) | gt[:, sl]) & allowed[:, sl]
        biases.append(jnp.where(sel, 0.0, MASK_NEG))
        carry = carry + jnp.sum(tief[:, sl], axis=1, keepdims=True)
    bias = jnp.concatenate(biases, axis=1)
    bias_all = jnp.concatenate([bias] * C_HEADS, axis=0)

    zq = jnp.zeros((groups * ds, hd), F32)
    qrows = []
    for hk in range(C_KV_HEADS):
        qg = jnp.concatenate([q[:, (hk * groups + g) * hd:(hk * groups + g + 1) * hd] for g in range(groups)], axis=0)
        qrows.append(jnp.concatenate([qg if c == hk else zq for c in range(C_KV_HEADS)], axis=1))
    qd = jnp.concatenate(qrows, axis=0).astype(BF16)
    s = jnp.dot(qd, kcat[...], preferred_element_type=F32) + bias_all
    m = jnp.max(s, axis=1, keepdims=True)
    p = jnp.exp(s - m)
    l = jnp.sum(p, axis=1, keepdims=True)
    og = _nt(p.astype(BF16), vcat[...]) / l
    outs = []
    for hk in range(C_KV_HEADS):
        for g in range(groups):
            r0 = (hk * groups + g) * ds
            outs.append(og[r0:r0 + ds, hk * hd:(hk + 1) * hd])
    o = jnp.concatenate(outs, axis=1)
    o_ref[...] = o * (cz * _sigmoid(cz))


def _dsa_decode(u, cache_k, cache_v, cache_kidx, page_table, *, layer, nb, ds, n_sel):
    npages = page_table.shape[1]
    pt = page_table.reshape(-1)

    def page(j, width):
        return pl.BlockSpec((None, None, width, PAGE_SIZE), lambda b, pt_ref: (layer, pt_ref[b * npages + j], 0, 0))

    in_specs = [
        pl.BlockSpec((ds, 512), lambda b, pt_ref: (b, BLK_CQ)),
        pl.BlockSpec((ds, 256), lambda b, pt_ref: (b, BLK256_CQI)),
        pl.BlockSpec((ds, 512), lambda b, pt_ref: (b, BLK_CZ)),
        pl.BlockSpec((ds, 256), lambda b, pt_ref: (b, BLK256_KV)),
        pl.BlockSpec((ds, LANES), lambda b, pt_ref: (b, BLK_KIW)),
    ]
    args = [pt, u, u, u, u, u]
    for arr, width in ((cache_k, LANES), (cache_v, LANES), (cache_kidx, IDX_DIM)):
        in_specs += [page(j, width) for j in range(npages)]
        args += [arr] * npages
    grid_spec = pltpu.PrefetchScalarGridSpec(
        num_scalar_prefetch=1,
        grid=(nb,),
        in_specs=in_specs,
        out_specs=pl.BlockSpec((ds, C_WIDTH), lambda b, pt_ref: (b, 0)),
        scratch_shapes=[pltpu.VMEM((C_KV_HEADS * C_HEAD_DIM, (npages + 1) * LANES), BF16),
                        pltpu.VMEM((C_KV_HEADS * C_HEAD_DIM, (npages + 1) * LANES), BF16),
                        pltpu.VMEM((IDX_DIM, (npages + 1) * LANES), BF16)],
    )
    return pl.pallas_call(
        functools.partial(_dsa_decode_kernel, n_sel=n_sel, npages=npages, ds=ds),
        grid_spec=grid_spec,
        out_shape=jax.ShapeDtypeStruct((nb * ds, C_WIDTH), F32),
        compiler_params=_cparams(("arbitrary",)),
        name="dsa_decode",
    )(*args)


def _merge_kernel(h_ref, ya_ref, yb_ref, yc_ref, ga_ref, gb_ref, gc_ref, wpa_ref, wpb_ref, wpc_ref, wo_ref, fg_ref, o_ref, *, final):
    def proj(y_ref, w_ref):
        return jnp.dot(y_ref[...].astype(BF16), w_ref[...], preferred_element_type=F32)

    m = (_sigmoid(ga_ref[...]) * proj(ya_ref, wpa_ref)
         + _sigmoid(gb_ref[...]) * proj(yb_ref, wpb_ref)
         + _sigmoid(gc_ref[...]) * proj(yc_ref, wpc_ref))
    h = h_ref[...] + jnp.dot(m.astype(BF16), wo_ref[...], preferred_element_type=F32)
    if final:
        h = h * lax.rsqrt(jnp.mean(h * h, axis=-1, keepdims=True) + EPS) * fg_ref[...]
    o_ref[...] = h


def _merge(h, ya, yb, yc, u, wpa, wpb, wpc, wo, fg, *, final):
    rows = h.shape[0]
    tm = next(t for t in (512, 256, 128, 64, 32, 16, 8) if rows % t == 0)

    def rowblk(width, blk=0):
        return pl.BlockSpec((tm, width), lambda i: (i, blk))

    def full(a):
        return pl.BlockSpec(a.shape, lambda i: (0,) * a.ndim)

    return pl.pallas_call(
        functools.partial(_merge_kernel, final=final),
        grid=(rows // tm,),
        in_specs=[rowblk(D_MODEL), rowblk(512), rowblk(512), rowblk(512),
                  rowblk(D_MODEL, BLK_GATE), rowblk(D_MODEL, BLK_GATE + 1), rowblk(D_MODEL, BLK_GATE + 2),
                  full(wpa), full(wpb), full(wpc), full(wo), full(fg)],
        out_specs=rowblk(D_MODEL),
        out_shape=jax.ShapeDtypeStruct((rows, D_MODEL), F32),
        compiler_params=_cparams(("parallel",)),
        name="merge",
    )(h, ya, yb, yc, u, u, u, wpa, wpb, wpc, wo, fg)


def _rope_tables(pos):
    half = ROT_DIM // 2
    inv = ROPE_THETA ** (-jnp.arange(half, dtype=F32) * 2.0 / ROT_DIM)
    ang = pos.astype(F32)[:, None] * jnp.tile(inv, LANES // half)[None, :]
    return jnp.cos(ang), jnp.sin(ang)


def _cache_rows(u, nb, t_pad, t_valid):
    u3 = u.reshape(nb, t_pad, U_W)[:, :t_valid]
    k0, ki0 = BLK256_KV * 256, BLK_KIW * LANES
    heads = (nb, t_valid, C_KV_HEADS, C_HEAD_DIM)
    return (u3[..., k0:k0 + LANES].reshape(heads), u3[..., k0 + LANES:k0 + 2 * LANES].reshape(heads),
            u3[..., ki0:ki0 + IDX_DIM])


def _pack_cols(w):
    split = 4608
    tail = split + 68
    pad = jnp.zeros(w.shape[:-1] + (U_W - N_IN,), w.dtype)
    return jnp.concatenate([w[..., :split], w[..., tail:], w[..., split:tail], pad], axis=-1)


def kernel(x_prompt, x_sample, cache_k, cache_v, cache_kidx, state_hgrn, state_conv, page_table, meta_tokens, norm_g, w_in, b_in, lb_logits, hgrn_norm_g, conv_w, conv_b, conv_ln_g, conv_ln_b, conv_pw, w_pa, w_pb, w_pc, w_out, final_norm_g):
    nbp, seq, _ = x_prompt.shape
    nbs, ds, _ = x_sample.shape
    depth = w_in.shape[0]
    npages = page_table.shape[1]
    past = npages * PAGE_SIZE
    t_valid = seq + N_META
    t_pad = -(-t_valid // LANES) * LANES
    n_sel_p = min(TOPK_MAX, t_valid // 4)
    n_sel_s = min(TOPK_MAX, (past + ds) // 4)
    n_phys = cache_k.shape[1]

    meta = jnp.broadcast_to(meta_tokens[None].astype(F32), (nbp, N_META, D_MODEL))
    hp = jnp.concatenate([meta, x_prompt, jnp.zeros((nbp, t_pad - t_valid, D_MODEL), F32)], axis=1).reshape(nbp * t_pad, D_MODEL)
    hs = x_sample.reshape(nbs * ds, D_MODEL)
    cos_p, sin_p = _rope_tables(jnp.tile(jnp.arange(t_pad), nbp))
    cos_s, sin_s = _rope_tables(jnp.tile(past + jnp.arange(ds), nbs))

    w_in_p = _pack_cols(w_in).astype(BF16)
    b_in_p = _pack_cols(b_in).reshape(depth, 1, U_W)
    ck4 = cache_k.transpose(0, 1, 3, 4, 2).reshape(depth, n_phys, C_KV_HEADS * C_HEAD_DIM, PAGE_SIZE)
    cv4 = cache_v.transpose(0, 1, 3, 4, 2).reshape(depth, n_phys, C_KV_HEADS * C_HEAD_DIM, PAGE_SIZE)
    ci4 = cache_kidx.transpose(0, 1, 3, 2)
    fg = final_norm_g.reshape(1, D_MODEL)

    outs = {k: [] for k in ("pk", "pv", "pki", "ps", "pc", "sk", "sv", "ski", "ss", "sc")}
    for l in range(depth):
        g = norm_g[l].reshape(1, D_MODEL)
        gn = hgrn_norm_g[l].reshape(1, A_HEAD)
        cw, cb = conv_w[l], conv_b[l].reshape(1, B_WIDTH)
        lng, lnb = conv_ln_g[l].reshape(1, B_WIDTH), conv_ln_b[l].reshape(1, B_WIDTH)
        pw = conv_pw[l].astype(BF16)
        wpa, wpb, wpc, wo = (w[l].astype(BF16) for w in (w_pa, w_pb, w_pc, w_out))
        final = l == depth - 1

        u = _inproj(hp, g, w_in_p[l], b_in_p[l], cos_p, sin_p)
        ya, s_new = _hgrn_chunked(u, lb_logits, gn, layer=l, nb=nbp, t_pad=t_pad, t_valid=t_valid)
        yb, c_new = _conv_prompt(u, cw, cb, lng, lnb, pw, nb=nbp, t_pad=t_pad, t_valid=t_valid)
        yc = _dsa_tiles(u, nb=nbp, t_pad=t_pad, n_sel=n_sel_p)
        hp = _merge(hp, ya, yb, yc, u, wpa, wpb, wpc, wo, fg, final=final)
        k_new, v_new, ki_new = _cache_rows(u, nbp, t_pad, t_valid)
        for name, val in zip(("pk", "pv", "pki", "ps", "pc"), (k_new, v_new, ki_new, s_new, c_new)):
            outs[name].append(val)

        u = _inproj(hs, g, w_in_p[l], b_in_p[l], cos_s, sin_s)
        ya, s_new = _hgrn(u, lb_logits, gn, state_hgrn, layer=l, nb=nbs, t_pad=ds, t_valid=ds, rb=ds, sb=ds, state_layer=l)
        yb, c_new = _conv_decode(u, state_conv, cw, cb, lng, lnb, pw, layer=l, nb=nbs, ds=ds)
        yc = _dsa_decode(u, ck4, cv4, ci4, page_table, layer=l, nb=nbs, ds=ds, n_sel=n_sel_s)
        hs = _merge(hs, ya, yb, yc, u, wpa, wpb, wpc, wo, fg, final=final)
        k_new, v_new, ki_new = _cache_rows(u, nbs, ds, ds)
        for name, val in zip(("sk", "sv", "ski", "ss", "sc"), (k_new, v_new, ki_new, s_new, c_new)):
            outs[name].append(val)

    y_prompt = hp.reshape(nbp, t_pad, D_MODEL)[:, N_META:t_valid]
    y_sample = hs.reshape(nbs, ds, D_MODEL)
    st = {k: jnp.stack(v) for k, v in outs.items()}
    return (y_prompt, y_sample, st["pk"], st["pv"], st["pki"], st["ps"], st["pc"],
            st["sk"], st["sv"], st["ski"], st["ss"], st["sc"])
```

```python
import functools

import numpy as np
import jax
import jax.numpy as jnp
from jax import lax
from jax.experimental import pallas as pl
from jax.experimental.pallas import tpu as pltpu

F32 = jnp.float32
BF16 = jnp.bfloat16
I32 = jnp.int32
I16 = jnp.int16

D_MODEL = 1024
N_META = 16
EPS = 1e-6
MASK_NEG = -1e30
A_WIDTH = 512
A_HEAD = 128
A_HEADS = 4
B_WIDTH = 512
CONV_W = 31
C_HEADS = 8
C_HEAD_DIM = 64
C_WIDTH = 512
C_KV_HEADS = 2
IDX_HEADS = 4
IDX_DIM = 64
TOPK_MAX = 256
PAGE_SIZE = 128
ROPE_THETA = 500000.0
ROT_DIM = 16
IDX_SCALE = (IDX_HEADS * IDX_DIM) ** -0.5
INT_MIN = -(2 ** 31)

LANES = 128
SUBLANES = 8
VMEM_LIMIT = 48 * 1024 * 1024

N_IN = 8260
U_W = 8448
TN_IN = 768
BLK_AQ, BLK_AF, BLK_AI, BLK_AZ, BLK_GLUA, BLK_GLUG, BLK_BZ, BLK_CQ, BLK_KVQI, BLK_CZ = range(10)
BLK_GATE = 5
BLK_KIW = 64
BLK256_KV = 16
BLK256_CQI = 17


def _cparams(sem):
    return pltpu.CompilerParams(dimension_semantics=sem, vmem_limit_bytes=VMEM_LIMIT)


def _sigmoid(x):
    return 1.0 / (1.0 + jnp.exp(-x))


def _nt(a, b):
    return lax.dot_general(a, b, (((1,), (1,)), ((), ())), preferred_element_type=F32)


def _tn(a, b):
    return lax.dot_general(a, b, (((0,), (0,)), ((), ())), preferred_element_type=F32)


def _sort_key(x):
    bits = pltpu.bitcast(x, I32)
    return bits ^ ((bits >> 31) & jnp.int32(0x7FFFFFFF))


ROPE_TWO_HEADS = (28, 29, 30, 31, 32, 34, 35)
ROPE_ONE_HEAD = (BLK_KIW,)


def _inproj_kernel(x_ref, g_ref, w_ref, b_ref, cos_ref, sin_ref, o_ref, xn_ref):
    j = pl.program_id(1)

    @pl.when(j == 0)
    def _():
        x = x_ref[...]
        ms = jnp.mean(x * x, axis=-1, keepdims=True)
        xn_ref[...] = (x * lax.rsqrt(ms + EPS) * g_ref[...]).astype(BF16)

    o_ref[...] = jnp.dot(xn_ref[...], w_ref[...], preferred_element_type=F32) + b_ref[...]

    per_tile = TN_IN // LANES
    half = ROT_DIM // 2
    lane = lax.broadcasted_iota(I32, (1, LANES), 1)
    for tile in sorted({blk // per_tile for blk in ROPE_TWO_HEADS + ROPE_ONE_HEAD}):
        @pl.when(j == tile)
        def _(tile=tile):
            cosd = cos_ref[...]
            sind = sin_ref[...]
            for blk in ROPE_TWO_HEADS + ROPE_ONE_HEAD:
                if blk // per_tile != tile:
                    continue
                pos = lane % C_HEAD_DIM if blk in ROPE_TWO_HEADS else lane
                c = jnp.where(pos < ROT_DIM, cosd, 1.0)
                a = jnp.where((pos >= half) & (pos < ROT_DIM), sind, 0.0)
                b = jnp.where(pos < half, -sind, 0.0)
                sl = slice((blk % per_tile) * LANES, (blk % per_tile + 1) * LANES)
                x = o_ref[:, sl]
                o_ref[:, sl] = x * c + pltpu.roll(x, half, 1) * a + pltpu.roll(x, LANES - half, 1) * b


def _inproj(h, g, w, b, cosd, sind):
    rows = h.shape[0]
    tm = next(t for t in (1024, 512, 256, 128, 64, 32, 16, 8) if rows % t == 0)
    return pl.pallas_call(
        _inproj_kernel,
        grid=(rows // tm, U_W // TN_IN),
        in_specs=[
            pl.BlockSpec((tm, D_MODEL), lambda i, j: (i, 0)),
            pl.BlockSpec((1, D_MODEL), lambda i, j: (0, 0)),
            pl.BlockSpec((D_MODEL, TN_IN), lambda i, j: (0, j)),
            pl.BlockSpec((1, TN_IN), lambda i, j: (0, j)),
            pl.BlockSpec((tm, LANES), lambda i, j: (i, 0)),
            pl.BlockSpec((tm, LANES), lambda i, j: (i, 0)),
        ],
        out_specs=pl.BlockSpec((tm, TN_IN), lambda i, j: (i, j)),
        out_shape=jax.ShapeDtypeStruct((rows, U_W), F32),
        scratch_shapes=[pltpu.VMEM((tm, D_MODEL), BF16)],
        compiler_params=_cparams(("parallel", "arbitrary")),
        name="inproj",
    )(h, g, w, b, cosd, sind)


def _hgrn_kernel(*refs, layer, sb, rb, t_valid, has_s0):
    aq_ref, af_ref, ai_ref, az_ref, lbl_ref, gn_ref = refs[:6]
    if has_s0:
        s0_ref, o_ref, sout_ref, st_ref = refs[6:]
    else:
        o_ref, sout_ref, st_ref = refs[6:]
    j = pl.program_id(1)
    nsub_max = rb // sb

    @pl.when(j == 0)
    def _():
        for h in range(A_HEADS):
            if has_s0:
                st_ref[h] = s0_ref[h].T
            else:
                st_ref[h] = jnp.zeros((A_HEAD, A_HEAD), F32)

    lg = lbl_ref[...]
    e = jnp.exp(lg - jnp.max(lg, axis=0, keepdims=True))
    sm = e / jnp.sum(e, axis=0, keepdims=True)
    lb = jnp.zeros((1, A_WIDTH), F32)
    for i in range(1, layer + 1):
        lb = lb + sm[i:i + 1, :]

    n_sub = jnp.clip((t_valid - j * rb) // sb, 0, nsub_max)
    if nsub_max > 1:
        o_ref[...] = jnp.zeros((rb, A_WIDTH), F32)
    ri = lax.broadcasted_iota(I32, (sb, sb), 0)
    ci = lax.broadcasted_iota(I32, (sb, sb), 1)
    tri = (ri >= ci).astype(F32)
    rowi = lax.broadcasted_iota(I32, (sb, 1), 0)
    gn = gn_ref[...]

    def sub(i, carry):
        r0 = pl.multiple_of(i * sb, sb)
        f = af_ref[pl.ds(r0, sb), :]
        aq = aq_ref[pl.ds(r0, sb), :]
        v = ai_ref[pl.ds(r0, sb), :]
        az = az_ref[pl.ds(r0, sb), :]
        logf = jnp.log(lb + (1.0 - lb) * _sigmoid(f))
        kk = (1.0 - lb) * _sigmoid(-f)
        q = aq * _sigmoid(aq)
        b = jnp.dot(tri, logf, preferred_element_type=F32, precision=lax.Precision.HIGHEST)
        blast = b[sb - 1:sb, :]
        qe = q * jnp.exp(b)
        kd = kk * jnp.exp(blast - b)
        dec = jnp.exp(blast)
        outs = []
        for h in range(A_HEADS):
            sl = slice(h * A_HEAD, (h + 1) * A_HEAD)
            st = st_ref[h]
            o = _nt(qe[:, sl].astype(BF16), st.astype(BF16))
            qh, kh, vh, bh = q[:, sl], kk[:, sl], v[:, sl], b[:, sl]
            for s in range(sb):
                w = jnp.exp(bh - bh[s:s + 1, :])
                a = jnp.sum(qh * (kh[s:s + 1, :] * w), axis=-1, keepdims=True)
                o = o + jnp.where(rowi >= s, a, 0.0) * vh[s:s + 1, :]
            st_ref[h] = st * dec[:, sl] + _tn(vh.astype(BF16), kd[:, sl].astype(BF16))
            on = o * lax.rsqrt(jnp.mean(o * o, axis=-1, keepdims=True) + EPS) * gn
            outs.append(on)
        o_ref[pl.ds(r0, sb), :] = jnp.concatenate(outs, axis=1) * (az * _sigmoid(az))
        return carry

    lax.fori_loop(0, n_sub, sub, 0)

    @pl.when(j == pl.num_programs(1) - 1)
    def _():
        for h in range(A_HEADS):
            sout_ref[h] = st_ref[h].T


def _hgrn(u, lb_logits, gn_g, s0, *, layer, nb, t_pad, t_valid, rb, sb, state_layer=None):
    nj = t_pad // rb
    has_s0 = s0 is not None

    def ublk(blk):
        return pl.BlockSpec((rb, 512), lambda b, j: (b * nj + j, blk))

    in_specs = [ublk(BLK_AQ), ublk(BLK_AF), ublk(BLK_AI), ublk(BLK_AZ),
                pl.BlockSpec(lb_logits.shape, lambda b, j: (0, 0)),
                pl.BlockSpec((1, A_HEAD), lambda b, j: (0, 0))]
    args = [u, u, u, u, lb_logits, gn_g]
    if has_s0:
        in_specs.append(pl.BlockSpec((None, None, A_HEADS, A_HEAD, A_HEAD), lambda b, j: (state_layer, b, 0, 0, 0)))
        args.append(s0)
    return pl.pallas_call(
        functools.partial(_hgrn_kernel, layer=layer, sb=sb, rb=rb, t_valid=t_valid, has_s0=has_s0),
        grid=(nb, nj),
        in_specs=in_specs,
        out_specs=[pl.BlockSpec((rb, A_WIDTH), lambda b, j: (b * nj + j, 0)),
                   pl.BlockSpec((None, A_HEADS, A_HEAD, A_HEAD), lambda b, j: (b, 0, 0, 0))],
        out_shape=[jax.ShapeDtypeStruct((nb * t_pad, A_WIDTH), F32),
                   jax.ShapeDtypeStruct((nb, A_HEADS, A_HEAD, A_HEAD), F32)],
        scratch_shapes=[pltpu.VMEM((A_HEADS, A_HEAD, A_HEAD), F32)],
        compiler_params=_cparams(("parallel", "arbitrary")),
        name="hgrn",
    )(*args)


def _hgrn_levels(c):
    out, size = [], SUBLANES
    while size < c:
        out.append(size)
        size *= 2
    return out


def _hgrn_chunk_kernel(aq_ref, af_ref, ai_ref, az_ref, lbl_ref, gn_ref, o_ref, sout_ref,
                       st_ref, q_s, k_s, b_s, oi_s, dg_s, qf_s, kf_s, mask_s, *, layer, c, t_valid):
    j = pl.program_id(1)
    levels = _hgrn_levels(c)
    ri = lax.broadcasted_iota(I32, (c, c), 0)
    ci = lax.broadcasted_iota(I32, (c, c), 1)

    @pl.when(j == 0)
    def _():
        for h in range(A_HEADS):
            st_ref[h] = jnp.zeros((A_HEAD, A_HEAD), F32)
        for n, size in enumerate(levels):
            same = (ri // (2 * size)) == (ci // (2 * size))
            mask_s[n] = jnp.where(same & ((ri % (2 * size)) >= size) & ((ci % (2 * size)) < size), 1.0, 0.0)

    lg = lbl_ref[...]
    e = jnp.exp(lg - jnp.max(lg, axis=0, keepdims=True))
    sm = e / jnp.sum(e, axis=0, keepdims=True)
    lb = jnp.zeros((1, A_WIDTH), F32)
    for i in range(1, layer + 1):
        lb = lb + sm[i:i + 1, :]

    valid = (j * c + lax.broadcasted_iota(I32, (c, 1), 0)) < t_valid
    f = af_ref[...]
    aq = aq_ref[...]
    logf = jnp.where(valid, jnp.log(lb + (1.0 - lb) * _sigmoid(f)), 0.0)
    kk = jnp.where(valid, (1.0 - lb) * _sigmoid(-f), 0.0)
    q = aq * _sigmoid(aq)
    b = jnp.dot((ri >= ci).astype(F32), logf, preferred_element_type=F32, precision=lax.Precision.HIGHEST)
    blast = b[c - 1:c, :]
    q_s[...] = q
    k_s[...] = kk
    b_s[...] = b
    for n, size in enumerate(levels):
        groups = c // (2 * size)
        bref = jnp.broadcast_to(b.reshape(groups, 2 * size, A_WIDTH)[:, size - 1:size, :], (groups, 2 * size, A_WIDTH))
        fac = jnp.exp(-jnp.abs(b - bref.reshape(c, A_WIDTH)))
        qf_s[n] = (q * fac).astype(BF16)
        kf_s[n] = (kk * fac).astype(BF16)
    qe = (q * jnp.exp(b)).astype(BF16)
    kd = (kk * jnp.exp(blast - b)).astype(BF16)
    dec = jnp.exp(blast)
    vb = ai_ref[...].astype(BF16)
    for h in range(A_HEADS):
        sl = slice(h * A_HEAD, (h + 1) * A_HEAD)
        st = st_ref[h]
        o = _nt(qe[:, sl], st.astype(BF16))
        if levels:
            att = jnp.zeros((c, c), F32)
            for n in range(len(levels)):
                att = att + _nt(qf_s[n, :, sl], kf_s[n, :, sl]) * mask_s[n]
            o = o + jnp.dot(att.astype(BF16), vb[:, sl], preferred_element_type=F32)
        oi_s[:, sl] = o
        st_ref[h] = st * dec[:, sl] + _tn(vb[:, sl], kd[:, sl])

    row8 = lax.broadcasted_iota(I32, (SUBLANES, 1), 0)
    for x in range(c // SUBLANES):
        rows = slice(x * SUBLANES, (x + 1) * SUBLANES)
        outs = []
        for h in range(A_HEADS):
            sl = slice(h * A_HEAD, (h + 1) * A_HEAD)
            qh, kh, bh, vh = q_s[rows, sl], k_s[rows, sl], b_s[rows, sl], ai_ref[rows, sl]
            o = jnp.zeros((SUBLANES, A_HEAD), F32)
            for s in range(SUBLANES):
                w = jnp.exp(bh - bh[s:s + 1, :])
                a = jnp.sum(qh * (kh[s:s + 1, :] * w), axis=-1, keepdims=True)
                o = o + jnp.where(row8 >= s, a, 0.0) * vh[s:s + 1, :]
            outs.append(o)
        dg_s[rows, :] = jnp.concatenate(outs, axis=1)

    gn = gn_ref[...]
    az = az_ref[...]
    outs = []
    for h in range(A_HEADS):
        sl = slice(h * A_HEAD, (h + 1) * A_HEAD)
        o = oi_s[:, sl] + dg_s[:, sl]
        outs.append(o * lax.rsqrt(jnp.mean(o * o, axis=-1, keepdims=True) + EPS) * gn)
    o_ref[...] = jnp.concatenate(outs, axis=1) * (az * _sigmoid(az))

    @pl.when(j == pl.num_programs(1) - 1)
    def _():
        for h in range(A_HEADS):
            sout_ref[h] = st_ref[h].T


def _hgrn_chunked(u, lb_logits, gn_g, *, layer, nb, t_pad, t_valid):
    c = LANES
    nj = t_pad // c
    nlev = max(len(_hgrn_levels(c)), 1)

    def ublk(blk):
        return pl.BlockSpec((c, 512), lambda b, j: (b * nj + j, blk))

    return pl.pallas_call(
        functools.partial(_hgrn_chunk_kernel, layer=layer, c=c, t_valid=t_valid),
        grid=(nb, nj),
        in_specs=[ublk(BLK_AQ), ublk(BLK_AF), ublk(BLK_AI), ublk(BLK_AZ),
                  pl.BlockSpec(lb_logits.shape, lambda b, j: (0, 0)),
                  pl.BlockSpec((1, A_HEAD), lambda b, j: (0, 0))],
        out_specs=[pl.BlockSpec((c, A_WIDTH), lambda b, j: (b * nj + j, 0)),
                   pl.BlockSpec((None, A_HEADS, A_HEAD, A_HEAD), lambda b, j: (b, 0, 0, 0))],
        out_shape=[jax.ShapeDtypeStruct((nb * t_pad, A_WIDTH), F32),
                   jax.ShapeDtypeStruct((nb, A_HEADS, A_HEAD, A_HEAD), F32)],
        scratch_shapes=[pltpu.VMEM((A_HEADS, A_HEAD, A_HEAD), F32),
                        pltpu.VMEM((c, A_WIDTH), F32),
                        pltpu.VMEM((c, A_WIDTH), F32),
                        pltpu.VMEM((c, A_WIDTH), F32),
                        pltpu.VMEM((c, A_WIDTH), F32),
                        pltpu.VMEM((c, A_WIDTH), F32),
                        pltpu.VMEM((nlev, c, A_WIDTH), BF16),
                        pltpu.VMEM((nlev, c, A_WIDTH), BF16),
                        pltpu.VMEM((nlev, c, c), F32)],
        compiler_params=_cparams(("parallel", "arbitrary")),
        name="hgrn_chunked",
    )(u, u, u, u, lb_logits, gn_g)


def _conv_tail(y, bz, lng, lnb, pw_ref):
    mu = jnp.mean(y, axis=-1, keepdims=True)
    d = y - mu
    var = jnp.mean(d * d, axis=-1, keepdims=True)
    y = d * lax.rsqrt(var + EPS) * lng + lnb
    y = y * _sigmoid(y)
    z = jnp.dot(y.astype(BF16), pw_ref[...], preferred_element_type=F32)
    return z * (bz * _sigmoid(bz))


def _conv_prompt_kernel(a_ref, g_ref, bz_ref, w_ref, cb_ref, lng_ref, lnb_ref, pw_ref, o_ref, st_ref, xb_ref, y_ref, xs_ref, *, rb, tail_off):
    j = pl.program_id(1)
    halo = 32

    @pl.when(j == 0)
    def _():
        xb_ref[0:halo, :] = jnp.zeros((halo, B_WIDTH), F32)

    xb_ref[halo:halo + rb, :] = a_ref[...] * _sigmoid(g_ref[...])
    span = rb + halo - SUBLANES
    for s in range(1, SUBLANES):
        xs_ref[s - 1, 0:span, :] = xb_ref[pl.ds(s, span), :]
    rc = 64
    for c in range(B_WIDTH // LANES):
        cs = slice(c * LANES, (c + 1) * LANES)
        wc = w_ref[:, cs]
        for r in range(rb // rc):
            acc = jnp.zeros((rc, LANES), F32) + cb_ref[:, cs]
            for t in range(CONV_W):
                off = halo - (CONV_W - 1) + t
                s, base = off % SUBLANES, (off // SUBLANES) * SUBLANES + r * rc
                win = xb_ref[base:base + rc, cs] if s == 0 else xs_ref[s - 1, base:base + rc, cs]
                acc = acc + win * wc[t:t + 1, :]
            y_ref[r * rc:(r + 1) * rc, cs] = acc
    o_ref[...] = _conv_tail(y_ref[...], bz_ref[...], lng_ref[...], lnb_ref[...], pw_ref)

    @pl.when(j == pl.num_programs(1) - 1)
    def _():
        st_ref[...] = xb_ref[pl.ds(tail_off, CONV_W - 1), :]

    xb_ref[0:halo, :] = xb_ref[rb:rb + halo, :]


def _conv_prompt(u, cw, cb, lng, lnb, pw, *, nb, t_pad, t_valid):
    rb = 128
    nj = t_pad // rb
    tail_off = (t_valid - (CONV_W - 1)) - ((nj - 1) * rb - 32)
    assert 0 <= tail_off and tail_off + CONV_W - 1 <= rb + 32

    def ublk(blk):
        return pl.BlockSpec((rb, 512), lambda b, j: (b * nj + j, blk))

    def full(a):
        return pl.BlockSpec(a.shape, lambda b, j: (0,) * a.ndim)

    return pl.pallas_call(
        functools.partial(_conv_prompt_kernel, rb=rb, tail_off=tail_off),
        grid=(nb, nj),
        in_specs=[ublk(BLK_GLUA), ublk(BLK_GLUG), ublk(BLK_BZ), full(cw), full(cb), full(lng), full(lnb), full(pw)],
        out_specs=[pl.BlockSpec((rb, B_WIDTH), lambda b, j: (b * nj + j, 0)),
                   pl.BlockSpec((None, CONV_W - 1, B_WIDTH), lambda b, j: (b, 0, 0))],
        out_shape=[jax.ShapeDtypeStruct((nb * t_pad, B_WIDTH), F32),
                   jax.ShapeDtypeStruct((nb, CONV_W - 1, B_WIDTH), F32)],
        scratch_shapes=[pltpu.VMEM((rb + 32, B_WIDTH), F32), pltpu.VMEM((rb, B_WIDTH), F32),
                        pltpu.VMEM((SUBLANES - 1, rb + 32, B_WIDTH), F32)],
        compiler_params=_cparams(("parallel", "arbitrary")),
        name="conv_prompt",
    )(u, u, u, cw, cb, lng, lnb, pw)


def _conv_decode_kernel(a_ref, g_ref, bz_ref, sin_ref, w_ref, cb_ref, lng_ref, lnb_ref, pw_ref, o_ref, sout_ref, xs_ref, *, ns, ds):
    nbuf = CONV_W - 1
    xs_ref[:, 0:nbuf, :] = sin_ref[...]
    glu = a_ref[...] * _sigmoid(g_ref[...])
    xs_ref[:, nbuf:nbuf + ds, :] = glu.reshape(ns, ds, B_WIDTH)
    acc = jnp.zeros((ns, ds, B_WIDTH), F32) + cb_ref[...]
    for t in range(CONV_W):
        acc = acc + xs_ref[:, pl.ds(t, ds), :] * w_ref[t:t + 1, :]
    o_ref[...] = _conv_tail(acc.reshape(ns * ds, B_WIDTH), bz_ref[...], lng_ref[...], lnb_ref[...], pw_ref)
    sout_ref[...] = xs_ref[:, pl.ds(ds, nbuf), :]


def _conv_decode(u, state, cw, cb, lng, lnb, pw, *, layer, nb, ds):
    ns = next(t for t in (16, 8, 4, 2, 1) if nb % t == 0)
    rb = ns * ds

    def ublk(blk):
        return pl.BlockSpec((rb, 512), lambda i: (i, blk))

    def full(a):
        return pl.BlockSpec(a.shape, lambda i: (0,) * a.ndim)

    return pl.pallas_call(
        functools.partial(_conv_decode_kernel, ns=ns, ds=ds),
        grid=(nb // ns,),
        in_specs=[ublk(BLK_GLUA), ublk(BLK_GLUG), ublk(BLK_BZ),
                  pl.BlockSpec((None, ns, CONV_W - 1, B_WIDTH), lambda i: (layer, i, 0, 0)),
                  full(cw), full(cb), full(lng), full(lnb), full(pw)],
        out_specs=[pl.BlockSpec((rb, B_WIDTH), lambda i: (i, 0)),
                   pl.BlockSpec((ns, CONV_W - 1, B_WIDTH), lambda i: (i, 0, 0))],
        out_shape=[jax.ShapeDtypeStruct((nb * ds, B_WIDTH), F32),
                   jax.ShapeDtypeStruct((nb, CONV_W - 1, B_WIDTH), F32)],
        scratch_shapes=[pltpu.VMEM((ns, CONV_W - 1 + ds, B_WIDTH), F32)],
        compiler_params=_cparams(("parallel",)),
        name="conv_decode",
    )(u, u, u, state, cw, cb, lng, lnb, pw)


def _dsa_tile_kernel(q_ref, qi_ref, cz_ref, kv_ref, kiw_ref, o_ref,
                     k16, vt16, ki16, wq_s, ik, hi16, lo16, s_buf, acc_s, *, n_sel, nblk, nb1, qw, t_pad):
    i = pl.program_id(1)
    blk = LANES
    hd = C_HEAD_DIM
    vregs = blk // SUBLANES
    qblocks = qw // blk
    groups = C_HEADS // C_KV_HEADS
    gw = groups * qw

    @pl.when(i == 0)
    def _():
        for kb in range(nblk):
            sl = slice(kb * blk, (kb + 1) * blk)
            x = kv_ref[sl, :]
            for hk in range(C_KV_HEADS):
                k16[hk * nb1 + kb] = x[:, hk * hd:(hk + 1) * hd].astype(BF16)
            vt16[kb] = x[:, blk:2 * blk].T.astype(BF16)
            ki16[kb] = kiw_ref[sl, :][:, 0:IDX_DIM].astype(BF16)
        for kb in range(nblk, nb1):
            for hk in range(C_KV_HEADS):
                k16[hk * nb1 + kb] = jnp.zeros((blk, hd), BF16)
            vt16[kb] = jnp.zeros((blk, blk), BF16)
            ki16[kb] = jnp.zeros((blk, IDX_DIM), BF16)
        wq_s[0:t_pad, :] = kiw_ref[...]
        if nb1 * blk > t_pad:
            wq_s[t_pad:nb1 * blk, :] = jnp.zeros((nb1 * blk - t_pad, blk), F32)

    npair = ((i + 1) * qblocks + 1) // 2
    r0 = pl.multiple_of(i * qw, qw)
    wq = wq_s[pl.ds(r0, qw), :]
    w_t = jnp.concatenate([wq[a * blk:(a + 1) * blk, :].T for a in range(qblocks)], axis=1)
    qi = qi_ref[...].astype(BF16)
    rowi = lax.broadcasted_iota(I32, (blk, qw), 0)
    qpos = r0 + lax.broadcasted_iota(I32, (blk, qw), 1)

    qis = jnp.concatenate([qi[:, h * IDX_DIM:(h + 1) * IDX_DIM] for h in range(IDX_HEADS)], axis=0)
    w_h = [w_t[IDX_DIM + h:IDX_DIM + h + 1, :] * IDX_SCALE for h in range(IDX_HEADS)]

    def p1(pp, c):
        for u in range(2):
            kb = 2 * pp + u
            s = _nt(ki16[kb], qis)
            acc = jnp.zeros((blk, qw), F32)
            for h in range(IDX_HEADS):
                acc = acc + jnp.maximum(s[:, h * qw:(h + 1) * qw], 0.0) * w_h[h]
            allowed = (kb * blk + rowi) <= qpos
            key = jnp.where(allowed, _sort_key(acc), jnp.int32(INT_MIN))
            ik[kb] = key
            hi16[kb] = (key >> 16).astype(I16)
        return c

    lax.fori_loop(0, npair, p1, 0)

    one16 = jnp.ones((), BF16)
    zero16 = jnp.zeros((), BF16)

    def count16(ref, pred):
        def body(pp, c):
            parts = []
            for u in range(2):
                m = jnp.where(pred(ref[2 * pp + u]), one16, zero16).reshape(vregs // 2, 2 * SUBLANES, qw)
                parts += [m[j] for j in range(vregs // 2)]
            while len(parts) > 1:
                parts = [parts[j] + parts[j + 1] for j in range(0, len(parts), 2)]
            return c + parts[0]
        c = lax.fori_loop(0, npair, body, jnp.zeros((2 * SUBLANES, qw), BF16))
        return jnp.sum(c.astype(F32), axis=0, keepdims=True)

    def search16(ref, want):
        lo = jnp.full((1, qw), -(1 << 15), I32)
        zero = jnp.zeros((1, qw), I32)
        v = jnp.where(count16(ref, lambda k: k >= zero.astype(I16)) >= want, zero, lo)

        def bit_body(t, v):
            cand = v + lax.shift_left(jnp.int32(1), jnp.int32(14) - t)
            return jnp.where(count16(ref, lambda k: k >= cand.astype(I16)) >= want, cand, v)

        return lax.fori_loop(0, 15, bit_body, v)

    thr_hi = search16(hi16, jnp.float32(n_sel))
    thr_hi16 = thr_hi.astype(I16)
    want_lo = n_sel - count16(hi16, lambda k: k > thr_hi16)

    def p2(pp, c):
        for u in range(2):
            kb = 2 * pp + u
            low = (ik[kb] & jnp.int32(0xFFFF)) - jnp.int32(1 << 15)
            lo16[kb] = jnp.where((ik[kb] >> 16) == thr_hi, low, jnp.int32(-(1 << 15))).astype(I16)
        return c

    lax.fori_loop(0, npair, p2, 0)
    thr_lo = search16(lo16, want_lo)
    thr = thr_hi * jnp.int32(1 << 16) + (thr_lo + jnp.int32(1 << 15))

    def count32(pred):
        def body(pp, c):
            for u in range(2):
                m = jnp.where(pred(ik[2 * pp + u]), 1, 0)
                c = c + jnp.sum(m.reshape(vregs, SUBLANES, qw), axis=0)
            return c
        c = lax.fori_loop(0, npair, body, jnp.zeros((SUBLANES, qw), I32))
        return jnp.sum(c, axis=0, keepdims=True)

    need = (n_sel - count32(lambda k: k > thr)).astype(F32)

    ltri = (lax.broadcasted_iota(I32, (blk, blk), 0) >= lax.broadcasted_iota(I32, (blk, blk), 1)).astype(BF16)
    q = (q_ref[...] * C_HEAD_DIM ** -0.5).astype(BF16)
    qg = [jnp.concatenate([q[:, (hk * groups + g) * hd:(hk * groups + g + 1) * hd] for g in range(groups)], axis=0)
          for hk in range(C_KV_HEADS)]

    def p3(pp, carry):
        tcar = carry[0]
        mx = list(carry[1:])
        for u in range(2):
            kb = 2 * pp + u
            key = ik[kb]
            tie = key == thr
            pre = jnp.dot(ltri, jnp.where(tie, 1.0, 0.0).astype(BF16), preferred_element_type=F32) + tcar
            allowed = (kb * blk + rowi) <= qpos
            sel = ((tie & (pre <= need)) | (key > thr)) & allowed
            bias = jnp.where(sel, 0.0, MASK_NEG)
            tcar = pre[blk - 1:blk, :]
            bias_g = jnp.concatenate([bias] * groups, axis=1)
            for hk in range(C_KV_HEADS):
                s = _nt(k16[hk * nb1 + kb], qg[hk]) + bias_g
                s_buf[hk * nb1 + kb] = s
                mx[hk] = jnp.maximum(mx[hk], jnp.max(s.reshape(vregs, SUBLANES, gw), axis=0))
        return (tcar, *mx)

    init = (jnp.zeros((1, qw), F32),) + tuple(jnp.full((SUBLANES, gw), -3.0e38, F32) for _ in range(C_KV_HEADS))
    res = lax.fori_loop(0, npair, p3, init)
    m = [jnp.max(res[1 + hk], axis=0, keepdims=True) for hk in range(C_KV_HEADS)]

    acc_s[...] = jnp.zeros((C_KV_HEADS, hd, gw), F32)

    def p4(pp, carry):
        ls = list(carry)
        for u in range(2):
            kb = 2 * pp + u
            vt = vt16[kb]
            for hk in range(C_KV_HEADS):
                p = jnp.exp(s_buf[hk * nb1 + kb] - m[hk])
                ls[hk] = ls[hk] + jnp.sum(p.reshape(vregs, SUBLANES, gw), axis=0)
                acc_s[hk] = acc_s[hk] + jnp.dot(vt[hk * hd:(hk + 1) * hd, :], p.astype(BF16), preferred_element_type=F32)
        return tuple(ls)

    ls = lax.fori_loop(0, npair, p4, tuple(jnp.zeros((SUBLANES, gw), F32) for _ in range(C_KV_HEADS)))
    outs = []
    for hk in range(C_KV_HEADS):
        og = acc_s[hk] / jnp.sum(ls[hk], axis=0, keepdims=True)
        outs += [og[:, g * qw:(g + 1) * qw] for g in range(groups)]
    o_t = jnp.concatenate(outs, axis=0)
    o = jnp.concatenate(
        [jnp.concatenate([o_t[j * blk:(j + 1) * blk, a * blk:(a + 1) * blk].T for j in range(C_WIDTH // blk)], axis=1)
         for a in range(qblocks)], axis=0)
    cz = cz_ref[...]
    o_ref[...] = o * (cz * _sigmoid(cz))


def _dsa_tiles(u, *, nb, t_pad, n_sel):
    blk = LANES
    qw = 2 * blk
    nblk = t_pad // blk
    nq = -(-t_pad // qw)
    nb1 = nq * (qw // blk)
    gw = (C_HEADS // C_KV_HEADS) * qw
    u3 = u.reshape(nb, t_pad, U_W)
    out = pl.pallas_call(
        functools.partial(_dsa_tile_kernel, n_sel=n_sel, nblk=nblk, nb1=nb1, qw=qw, t_pad=t_pad),
        grid=(nb, nq),
        in_specs=[
            pl.BlockSpec((None, qw, 512), lambda b, i: (b, i, BLK_CQ)),
            pl.BlockSpec((None, qw, 256), lambda b, i: (b, i, BLK256_CQI)),
            pl.BlockSpec((None, qw, 512), lambda b, i: (b, i, BLK_CZ)),
            pl.BlockSpec((None, t_pad, 256), lambda b, i: (b, 0, BLK256_KV)),
            pl.BlockSpec((None, t_pad, LANES), lambda b, i: (b, 0, BLK_KIW)),
        ],
        out_specs=pl.BlockSpec((None, qw, C_WIDTH), lambda b, i: (b, i, 0)),
        out_shape=jax.ShapeDtypeStruct((nb, t_pad, C_WIDTH), F32),
        scratch_shapes=[
            pltpu.VMEM((C_KV_HEADS * nb1, blk, C_HEAD_DIM), BF16),
            pltpu.VMEM((nb1, blk, blk), BF16),
            pltpu.VMEM((nb1, blk, IDX_DIM), BF16),
            pltpu.VMEM((nb1 * blk, LANES), F32),
            pltpu.VMEM((nb1, blk, qw), I32),
            pltpu.VMEM((nb1, blk, qw), I16),
            pltpu.VMEM((nb1, blk, qw), I16),
            pltpu.VMEM((C_KV_HEADS * nb1, blk, gw), F32),
            pltpu.VMEM((C_KV_HEADS, C_HEAD_DIM, gw), F32),
        ],
        compiler_params=_cparams(("parallel", "arbitrary")),
        name="dsa_tiles",
    )(u3, u3, u3, u3, u3)
    return out.reshape(nb * t_pad, C_WIDTH)


def _dsa_decode_kernel(pt_ref, q_ref, qi_ref, cz_ref, kvn_ref, kiwn_ref, ck_hbm, cv_hbm, ci_hbm, o_ref,
                       kcat, vcat, icat, kbuf, vbuf, ibuf, sems, *, layer, n_sel, npages, ds):
    b = pl.program_id(0)
    slot = b % 2
    sources = ((ck_hbm, kbuf), (cv_hbm, vbuf), (ci_hbm, ibuf))

    def page_copy(kind, page, to_slot, j):
        src, buf = sources[kind]
        return pltpu.make_async_copy(src.at[layer, page], buf.at[to_slot, j], sems.at[to_slot, kind])

    def fetch(seq, to_slot):
        for j in range(npages):
            page = pt_ref[seq * npages + j]
            for kind in range(len(sources)):
                page_copy(kind, page, to_slot, j).start()

    @pl.when(b == 0)
    def _():
        fetch(0, 0)

    for j in range(npages):
        for kind in range(len(sources)):
            page_copy(kind, 0, slot, j).wait()

    @pl.when(b + 1 < pl.num_programs(0))
    def _():
        fetch(b + 1, 1 - slot)

    kpages = [kbuf.at[slot, j] for j in range(npages)]
    vpages = [vbuf.at[slot, j] for j in range(npages)]
    ipages = [ibuf.at[slot, j] for j in range(npages)]
    q, qi, cz, kvn, kiwn = q_ref[...] * C_HEAD_DIM ** -0.5, qi_ref[...], cz_ref[...], kvn_ref[...], kiwn_ref[...]
    blk = LANES
    hd = C_HEAD_DIM
    nblk = npages + 1
    nkeys = nblk * blk
    groups = C_HEADS // C_KV_HEADS

    zpad = jnp.zeros((blk - ds, blk), F32)
    for j in range(npages):
        sl = slice(j * blk, (j + 1) * blk)
        kcat[:, sl] = kpages[j][...].astype(BF16)
        vcat[:, sl] = vpages[j][...].astype(BF16)
        icat[:, sl] = ipages[j][...].astype(BF16)
    new = slice(npages * blk, nkeys)
    kcat[:, new] = jnp.concatenate([kvn[:, 0:blk], zpad], axis=0).T.astype(BF16)
    vcat[:, new] = jnp.concatenate([kvn[:, blk:2 * blk], zpad], axis=0).T.astype(BF16)
    icat[:, new] = jnp.concatenate([kiwn, zpad], axis=0).T[0:IDX_DIM, :].astype(BF16)
    rowq = lax.broadcasted_iota(I32, (ds, nkeys), 0)
    kpos = lax.broadcasted_iota(I32, (ds, nkeys), 1)
    allowed = kpos <= npages * blk + rowq

    qis = jnp.concatenate([qi[:, h * IDX_DIM:(h + 1) * IDX_DIM] for h in range(IDX_HEADS)], axis=0).astype(BF16)
    wcol = jnp.concatenate([kiwn[:, IDX_DIM + h:IDX_DIM + h + 1] for h in range(IDX_HEADS)], axis=0) * IDX_SCALE
    r = jnp.maximum(jnp.dot(qis, icat[...], preferred_element_type=F32), 0.0) * wcol
    acc = r[0:ds]
    for h in range(1, IDX_HEADS):
        acc = acc + r[h * ds:(h + 1) * ds]
    keys = jnp.where(allowed, _sort_key(jnp.zeros((ds, nkeys), F32) + acc), jnp.int32(INT_MIN))

    def count_ge(cand):
        return jnp.sum(jnp.where(keys >= cand, 1, 0), axis=1, keepdims=True)

    thr = jnp.full((ds, 1), INT_MIN, I32)
    for step in range(8):
        shift = 28 - 4 * step
        digit = jnp.zeros((ds, 1), I32)
        for c in range(1, 16):
            inc = int(np.array(c << shift, dtype=np.uint32).astype(np.int32))
            digit = digit + jnp.where(count_ge(thr + jnp.int32(inc)) >= n_sel, 1, 0)
        thr = thr + digit * jnp.int32(1 << shift)

    gt = keys > thr
    tie = keys == thr
    need = (n_sel - jnp.sum(jnp.where(gt, 1, 0), axis=1, keepdims=True)).astype(F32)
    tief = jnp.where(tie, 1.0, 0.0)
    ri = lax.broadcasted_iota(I32, (blk, blk), 0)
    ci = lax.broadcasted_iota(I32, (blk, blk), 1)
    utri = (ri <= ci).astype(F32)
    carry = jnp.zeros((ds, 1), F32)
    biases = []
    for j in range(nblk):
        sl = slice(j * blk, (j + 1) * blk)
        pre = jnp.dot(tief[:, sl], utri, preferred_element_type=F32) + carry
        sel = ((tie[:, sl] & (pre <= need)) | gt[:, sl]) & allowed[:, sl]
        biases.append(jnp.where(sel, 0.0, MASK_NEG))
        carry = carry + jnp.sum(tief[:, sl], axis=1, keepdims=True)
    bias = jnp.concatenate(biases, axis=1)
    bias_all = jnp.concatenate([bias] * C_HEADS, axis=0)

    zq = jnp.zeros((groups * ds, hd), F32)
    qrows = []
    for hk in range(C_KV_HEADS):
        qg = jnp.concatenate([q[:, (hk * groups + g) * hd:(hk * groups + g + 1) * hd] for g in range(groups)], axis=0)
        qrows.append(jnp.concatenate([qg if c == hk else zq for c in range(C_KV_HEADS)], axis=1))
    qd = jnp.concatenate(qrows, axis=0).astype(BF16)
    s = jnp.dot(qd, kcat[...], preferred_element_type=F32) + bias_all
    m = jnp.max(s, axis=1, keepdims=True)
    p = jnp.exp(s - m)
    l = jnp.sum(p, axis=1, keepdims=True)
    og = _nt(p.astype(BF16), vcat[...]) / l
    outs = []
    for hk in range(C_KV_HEADS):
        for g in range(groups):
            r0 = (hk * groups + g) * ds
            outs.append(og[r0:r0 + ds, hk * hd:(hk + 1) * hd])
    o = jnp.concatenate(outs, axis=1)
    o_ref[...] = o * (cz * _sigmoid(cz))


def _dsa_decode(u, cache_k, cache_v, cache_kidx, page_table, *, layer, nb, ds, n_sel):
    npages = page_table.shape[1]
    pt = page_table.reshape(-1)
    kvw = C_KV_HEADS * C_HEAD_DIM
    hbm = pl.BlockSpec(memory_space=pl.ANY)
    grid_spec = pltpu.PrefetchScalarGridSpec(
        num_scalar_prefetch=1,
        grid=(nb,),
        in_specs=[
            pl.BlockSpec((ds, 512), lambda b, pt_ref: (b, BLK_CQ)),
            pl.BlockSpec((ds, 256), lambda b, pt_ref: (b, BLK256_CQI)),
            pl.BlockSpec((ds, 512), lambda b, pt_ref: (b, BLK_CZ)),
            pl.BlockSpec((ds, 256), lambda b, pt_ref: (b, BLK256_KV)),
            pl.BlockSpec((ds, LANES), lambda b, pt_ref: (b, BLK_KIW)),
            hbm, hbm, hbm,
        ],
        out_specs=pl.BlockSpec((ds, C_WIDTH), lambda b, pt_ref: (b, 0)),
        scratch_shapes=[pltpu.VMEM((kvw, (npages + 1) * LANES), BF16),
                        pltpu.VMEM((kvw, (npages + 1) * LANES), BF16),
                        pltpu.VMEM((IDX_DIM, (npages + 1) * LANES), BF16),
                        pltpu.VMEM((2, npages, kvw, PAGE_SIZE), F32),
                        pltpu.VMEM((2, npages, kvw, PAGE_SIZE), F32),
                        pltpu.VMEM((2, npages, IDX_DIM, PAGE_SIZE), F32),
                        pltpu.SemaphoreType.DMA((2, 3))],
    )
    return pl.pallas_call(
        functools.partial(_dsa_decode_kernel, layer=layer, n_sel=n_sel, npages=npages, ds=ds),
        grid_spec=grid_spec,
        out_shape=jax.ShapeDtypeStruct((nb * ds, C_WIDTH), F32),
        compiler_params=_cparams(("arbitrary",)),
        name="dsa_decode",
    )(pt, u, u, u, u, u, cache_k, cache_v, cache_kidx)


def _merge_kernel(h_ref, ya_ref, yb_ref, yc_ref, ga_ref, gb_ref, gc_ref, wpa_ref, wpb_ref, wpc_ref, wo_ref, fg_ref, o_ref, *, final):
    def proj(y_ref, w_ref):
        return jnp.dot(y_ref[...].astype(BF16), w_ref[...], preferred_element_type=F32)

    m = (_sigmoid(ga_ref[...]) * proj(ya_ref, wpa_ref)
         + _sigmoid(gb_ref[...]) * proj(yb_ref, wpb_ref)
         + _sigmoid(gc_ref[...]) * proj(yc_ref, wpc_ref))
    h = h_ref[...] + jnp.dot(m.astype(BF16), wo_ref[...], preferred_element_type=F32)
    if final:
        h = h * lax.rsqrt(jnp.mean(h * h, axis=-1, keepdims=True) + EPS) * fg_ref[...]
    o_ref[...] = h


def _merge(h, ya, yb, yc, u, wpa, wpb, wpc, wo, fg, *, final):
    rows = h.shape[0]
    tm = next(t for t in (512, 256, 128, 64, 32, 16, 8) if rows % t == 0)

    def rowblk(width, blk=0):
        return pl.BlockSpec((tm, width), lambda i: (i, blk))

    def full(a):
        return pl.BlockSpec(a.shape, lambda i: (0,) * a.ndim)

    return pl.pallas_call(
        functools.partial(_merge_kernel, final=final),
        grid=(rows // tm,),
        in_specs=[rowblk(D_MODEL), rowblk(512), rowblk(512), rowblk(512),
                  rowblk(D_MODEL, BLK_GATE), rowblk(D_MODEL, BLK_GATE + 1), rowblk(D_MODEL, BLK_GATE + 2),
                  full(wpa), full(wpb), full(wpc), full(wo), full(fg)],
        out_specs=rowblk(D_MODEL),
        out_shape=jax.ShapeDtypeStruct((rows, D_MODEL), F32),
        compiler_params=_cparams(("parallel",)),
        name="merge",
    )(h, ya, yb, yc, u, u, u, wpa, wpb, wpc, wo, fg)


def _rope_tables(pos):
    half = ROT_DIM // 2
    inv = ROPE_THETA ** (-jnp.arange(half, dtype=F32) * 2.0 / ROT_DIM)
    ang = pos.astype(F32)[:, None] * jnp.tile(inv, LANES // half)[None, :]
    return jnp.cos(ang), jnp.sin(ang)


def _cache_rows(u, nb, t_pad, t_valid):
    u3 = u.reshape(nb, t_pad, U_W)[:, :t_valid]
    k0, ki0 = BLK256_KV * 256, BLK_KIW * LANES
    heads = (nb, t_valid, C_KV_HEADS, C_HEAD_DIM)
    return (u3[..., k0:k0 + LANES].reshape(heads), u3[..., k0 + LANES:k0 + 2 * LANES].reshape(heads),
            u3[..., ki0:ki0 + IDX_DIM])


def _pack_cols(w):
    split = 4608
    tail = split + 68
    pad = jnp.zeros(w.shape[:-1] + (U_W - N_IN,), w.dtype)
    return jnp.concatenate([w[..., :split], w[..., tail:], w[..., split:tail], pad], axis=-1)


def kernel(x_prompt, x_sample, cache_k, cache_v, cache_kidx, state_hgrn, state_conv, page_table, meta_tokens, norm_g, w_in, b_in, lb_logits, hgrn_norm_g, conv_w, conv_b, conv_ln_g, conv_ln_b, conv_pw, w_pa, w_pb, w_pc, w_out, final_norm_g):
    nbp, seq, _ = x_prompt.shape
    nbs, ds, _ = x_sample.shape
    depth = w_in.shape[0]
    npages = page_table.shape[1]
    past = npages * PAGE_SIZE
    t_valid = seq + N_META
    t_pad = -(-t_valid // LANES) * LANES
    n_sel_p = min(TOPK_MAX, t_valid // 4)
    n_sel_s = min(TOPK_MAX, (past + ds) // 4)
    n_phys = cache_k.shape[1]

    meta = jnp.broadcast_to(meta_tokens[None].astype(F32), (nbp, N_META, D_MODEL))
    hp = jnp.concatenate([meta, x_prompt, jnp.zeros((nbp, t_pad - t_valid, D_MODEL), F32)], axis=1).reshape(nbp * t_pad, D_MODEL)
    hs = x_sample.reshape(nbs * ds, D_MODEL)
    cos_p, sin_p = _rope_tables(jnp.tile(jnp.arange(t_pad), nbp))
    cos_s, sin_s = _rope_tables(jnp.tile(past + jnp.arange(ds), nbs))

    w_in_p = _pack_cols(w_in).astype(BF16)
    b_in_p = _pack_cols(b_in).reshape(depth, 1, U_W)
    ck4 = cache_k.transpose(0, 1, 3, 4, 2).reshape(depth, n_phys, C_KV_HEADS * C_HEAD_DIM, PAGE_SIZE)
    cv4 = cache_v.transpose(0, 1, 3, 4, 2).reshape(depth, n_phys, C_KV_HEADS * C_HEAD_DIM, PAGE_SIZE)
    ci4 = cache_kidx.transpose(0, 1, 3, 2)
    fg = final_norm_g.reshape(1, D_MODEL)

    outs = {k: [] for k in ("pk", "pv", "pki", "ps", "pc", "sk", "sv", "ski", "ss", "sc")}
    for l in range(depth):
        g = norm_g[l].reshape(1, D_MODEL)
        gn = hgrn_norm_g[l].reshape(1, A_HEAD)
        cw, cb = conv_w[l], conv_b[l].reshape(1, B_WIDTH)
        lng, lnb = conv_ln_g[l].reshape(1, B_WIDTH), conv_ln_b[l].reshape(1, B_WIDTH)
        pw = conv_pw[l].astype(BF16)
        wpa, wpb, wpc, wo = (w[l].astype(BF16) for w in (w_pa, w_pb, w_pc, w_out))
        final = l == depth - 1

        u = _inproj(hp, g, w_in_p[l], b_in_p[l], cos_p, sin_p)
        ya, s_new = _hgrn_chunked(u, lb_logits, gn, layer=l, nb=nbp, t_pad=t_pad, t_valid=t_valid)
        yb, c_new = _conv_prompt(u, cw, cb, lng, lnb, pw, nb=nbp, t_pad=t_pad, t_valid=t_valid)
        yc = _dsa_tiles(u, nb=nbp, t_pad=t_pad, n_sel=n_sel_p)
        hp = _merge(hp, ya, yb, yc, u, wpa, wpb, wpc, wo, fg, final=final)
        k_new, v_new, ki_new = _cache_rows(u, nbp, t_pad, t_valid)
        for name, val in zip(("pk", "pv", "pki", "ps", "pc"), (k_new, v_new, ki_new, s_new, c_new)):
            outs[name].append(val)

        u = _inproj(hs, g, w_in_p[l], b_in_p[l], cos_s, sin_s)
        ya, s_new = _hgrn(u, lb_logits, gn, state_hgrn, layer=l, nb=nbs, t_pad=ds, t_valid=ds, rb=ds, sb=ds, state_layer=l)
        yb, c_new = _conv_decode(u, state_conv, cw, cb, lng, lnb, pw, layer=l, nb=nbs, ds=ds)
        yc = _dsa_decode(u, ck4, cv4, ci4, page_table, layer=l, nb=nbs, ds=ds, n_sel=n_sel_s)
        hs = _merge(hs, ya, yb, yc, u, wpa, wpb, wpc, wo, fg, final=final)
        k_new, v_new, ki_new = _cache_rows(u, nbs, ds, ds)
        for name, val in zip(("sk", "sv", "ski", "ss", "sc"), (k_new, v_new, ki_new, s_new, c_new)):
            outs[name].append(val)

    y_prompt = hp.reshape(nbp, t_pad, D_MODEL)[:, N_META:t_valid]
    y_sample = hs.reshape(nbs, ds, D_MODEL)
    st = {k: jnp.stack(v) for k, v in outs.items()}
    return (y_prompt, y_sample, st["pk"], st["pv"], st["pki"], st["ps"], st["pc"],
            st["sk"], st["sv"], st["ski"], st["ss"], st["sc"])
```

```python
import functools

import numpy as np
import jax
import jax.numpy as jnp
from jax import lax
from jax.experimental import pallas as pl
from jax.experimental.pallas import tpu as pltpu

F32 = jnp.float32
BF16 = jnp.bfloat16
I32 = jnp.int32
I16 = jnp.int16

D_MODEL = 1024
N_META = 16
EPS = 1e-6
MASK_NEG = -1e30
A_WIDTH = 512
A_HEAD = 128
A_HEADS = 4
B_WIDTH = 512
CONV_W = 31
C_HEADS = 8
C_HEAD_DIM = 64
C_WIDTH = 512
C_KV_HEADS = 2
IDX_HEADS = 4
IDX_DIM = 64
TOPK_MAX = 256
PAGE_SIZE = 128
ROPE_THETA = 500000.0
ROT_DIM = 16
IDX_SCALE = (IDX_HEADS * IDX_DIM) ** -0.5
INT_MIN = -(2 ** 31)

LANES = 128
SUBLANES = 8
VMEM_LIMIT = 48 * 1024 * 1024
VMEM_TILE_BUDGET = 24 * 1024 * 1024

N_IN = 8260
U_W = 8448
TN_IN = 768
BLK_AQ, BLK_AF, BLK_AI, BLK_AZ, BLK_GLUA, BLK_GLUG, BLK_BZ, BLK_CQ, BLK_KVQI, BLK_CZ = range(10)
BLK_GATE = 5
BLK_KIW = 64
BLK256_KV = 16
BLK256_CQI = 17


def _cparams(sem):
    return pltpu.CompilerParams(dimension_semantics=sem, vmem_limit_bytes=VMEM_LIMIT)


def _sigmoid(x):
    return 1.0 / (1.0 + jnp.exp(-x))


def _nt(a, b):
    return lax.dot_general(a, b, (((1,), (1,)), ((), ())), preferred_element_type=F32)


def _tn(a, b):
    return lax.dot_general(a, b, (((0,), (0,)), ((), ())), preferred_element_type=F32)


def _sort_key(x):
    bits = pltpu.bitcast(x, I32)
    return bits ^ ((bits >> 31) & jnp.int32(0x7FFFFFFF))


ROPE_TWO_HEADS = (28, 29, 30, 31, 32, 34, 35)
ROPE_ONE_HEAD = (BLK_KIW,)


def _inproj_kernel(x_ref, g_ref, w_ref, b_ref, cos_ref, sin_ref, o_ref, xn_ref):
    j = pl.program_id(1)

    @pl.when(j == 0)
    def _():
        x = x_ref[...]
        ms = jnp.mean(x * x, axis=-1, keepdims=True)
        xn_ref[...] = (x * lax.rsqrt(ms + EPS) * g_ref[...]).astype(BF16)

    o_ref[...] = jnp.dot(xn_ref[...], w_ref[...], preferred_element_type=F32) + b_ref[...]

    per_tile = TN_IN // LANES
    half = ROT_DIM // 2
    lane = lax.broadcasted_iota(I32, (1, LANES), 1)
    for tile in sorted({blk // per_tile for blk in ROPE_TWO_HEADS + ROPE_ONE_HEAD}):
        @pl.when(j == tile)
        def _(tile=tile):
            cosd = cos_ref[...]
            sind = sin_ref[...]
            for blk in ROPE_TWO_HEADS + ROPE_ONE_HEAD:
                if blk // per_tile != tile:
                    continue
                pos = lane % C_HEAD_DIM if blk in ROPE_TWO_HEADS else lane
                c = jnp.where(pos < ROT_DIM, cosd, 1.0)
                a = jnp.where((pos >= half) & (pos < ROT_DIM), sind, 0.0)
                b = jnp.where(pos < half, -sind, 0.0)
                sl = slice((blk % per_tile) * LANES, (blk % per_tile + 1) * LANES)
                x = o_ref[:, sl]
                o_ref[:, sl] = x * c + pltpu.roll(x, half, 1) * a + pltpu.roll(x, LANES - half, 1) * b


def _inproj_row_tile(rows):
    per_row = 2 * D_MODEL * 4 + 2 * TN_IN * 4 + D_MODEL * 2 + 4 * LANES * 4
    fixed = 2 * (D_MODEL * TN_IN * 2 + TN_IN * 4 + D_MODEL * 4)
    return max(t for t in range(SUBLANES, rows + 1, SUBLANES) if rows % t == 0 and t * per_row + fixed <= VMEM_TILE_BUDGET)


def _inproj(h, g, w, b, cosd, sind):
    rows = h.shape[0]
    tm = _inproj_row_tile(rows)
    return pl.pallas_call(
        _inproj_kernel,
        grid=(rows // tm, U_W // TN_IN),
        in_specs=[
            pl.BlockSpec((tm, D_MODEL), lambda i, j: (i, 0)),
            pl.BlockSpec((1, D_MODEL), lambda i, j: (0, 0)),
            pl.BlockSpec((None, D_MODEL, TN_IN), lambda i, j: (j, 0, 0)),
            pl.BlockSpec((1, TN_IN), lambda i, j: (0, j)),
            pl.BlockSpec((tm, LANES), lambda i, j: (i, 0)),
            pl.BlockSpec((tm, LANES), lambda i, j: (i, 0)),
        ],
        out_specs=pl.BlockSpec((tm, TN_IN), lambda i, j: (i, j)),
        out_shape=jax.ShapeDtypeStruct((rows, U_W), F32),
        scratch_shapes=[pltpu.VMEM((tm, D_MODEL), BF16)],
        compiler_params=_cparams(("parallel", "arbitrary")),
        name="inproj",
    )(h, g, w, b, cosd, sind)


def _hgrn_kernel(*refs, layer, sb, rb, t_valid, has_s0):
    aq_ref, af_ref, ai_ref, az_ref, lbl_ref, gn_ref = refs[:6]
    if has_s0:
        s0_ref, o_ref, sout_ref, st_ref = refs[6:]
    else:
        o_ref, sout_ref, st_ref = refs[6:]
    j = pl.program_id(1)
    nsub_max = rb // sb

    @pl.when(j == 0)
    def _():
        for h in range(A_HEADS):
            if has_s0:
                st_ref[h] = s0_ref[h].T
            else:
                st_ref[h] = jnp.zeros((A_HEAD, A_HEAD), F32)

    lg = lbl_ref[...]
    e = jnp.exp(lg - jnp.max(lg, axis=0, keepdims=True))
    sm = e / jnp.sum(e, axis=0, keepdims=True)
    lb = jnp.zeros((1, A_WIDTH), F32)
    for i in range(1, layer + 1):
        lb = lb + sm[i:i + 1, :]

    n_sub = jnp.clip((t_valid - j * rb) // sb, 0, nsub_max)
    if nsub_max > 1:
        o_ref[...] = jnp.zeros((rb, A_WIDTH), F32)
    ri = lax.broadcasted_iota(I32, (sb, sb), 0)
    ci = lax.broadcasted_iota(I32, (sb, sb), 1)
    tri = (ri >= ci).astype(F32)
    rowi = lax.broadcasted_iota(I32, (sb, 1), 0)
    gn = gn_ref[...]

    def sub(i, carry):
        r0 = pl.multiple_of(i * sb, sb)
        f = af_ref[pl.ds(r0, sb), :]
        aq = aq_ref[pl.ds(r0, sb), :]
        v = ai_ref[pl.ds(r0, sb), :]
        az = az_ref[pl.ds(r0, sb), :]
        logf = jnp.log(lb + (1.0 - lb) * _sigmoid(f))
        kk = (1.0 - lb) * _sigmoid(-f)
        q = aq * _sigmoid(aq)
        b = jnp.dot(tri, logf, preferred_element_type=F32, precision=lax.Precision.HIGHEST)
        blast = b[sb - 1:sb, :]
        qe = q * jnp.exp(b)
        kd = kk * jnp.exp(blast - b)
        dec = jnp.exp(blast)
        outs = []
        for h in range(A_HEADS):
            sl = slice(h * A_HEAD, (h + 1) * A_HEAD)
            st = st_ref[h]
            o = _nt(qe[:, sl].astype(BF16), st.astype(BF16))
            qh, kh, vh, bh = q[:, sl], kk[:, sl], v[:, sl], b[:, sl]
            for s in range(sb):
                w = jnp.exp(bh - bh[s:s + 1, :])
                a = jnp.sum(qh * (kh[s:s + 1, :] * w), axis=-1, keepdims=True)
                o = o + jnp.where(rowi >= s, a, 0.0) * vh[s:s + 1, :]
            st_ref[h] = st * dec[:, sl] + _tn(vh.astype(BF16), kd[:, sl].astype(BF16))
            on = o * lax.rsqrt(jnp.mean(o * o, axis=-1, keepdims=True) + EPS) * gn
            outs.append(on)
        o_ref[pl.ds(r0, sb), :] = jnp.concatenate(outs, axis=1) * (az * _sigmoid(az))
        return carry

    lax.fori_loop(0, n_sub, sub, 0)

    @pl.when(j == pl.num_programs(1) - 1)
    def _():
        for h in range(A_HEADS):
            sout_ref[h] = st_ref[h].T


def _hgrn(u, lb_logits, gn_g, s0, *, layer, nb, t_pad, t_valid, rb, sb, state_layer=None):
    nj = t_pad // rb
    has_s0 = s0 is not None

    def ublk(blk):
        return pl.BlockSpec((rb, 512), lambda b, j: (b * nj + j, blk))

    in_specs = [ublk(BLK_AQ), ublk(BLK_AF), ublk(BLK_AI), ublk(BLK_AZ),
                pl.BlockSpec(lb_logits.shape, lambda b, j: (0, 0)),
                pl.BlockSpec((1, A_HEAD), lambda b, j: (0, 0))]
    args = [u, u, u, u, lb_logits, gn_g]
    if has_s0:
        in_specs.append(pl.BlockSpec((None, None, A_HEADS, A_HEAD, A_HEAD), lambda b, j: (state_layer, b, 0, 0, 0)))
        args.append(s0)
    return pl.pallas_call(
        functools.partial(_hgrn_kernel, layer=layer, sb=sb, rb=rb, t_valid=t_valid, has_s0=has_s0),
        grid=(nb, nj),
        in_specs=in_specs,
        out_specs=[pl.BlockSpec((rb, A_WIDTH), lambda b, j: (b * nj + j, 0)),
                   pl.BlockSpec((None, A_HEADS, A_HEAD, A_HEAD), lambda b, j: (b, 0, 0, 0))],
        out_shape=[jax.ShapeDtypeStruct((nb * t_pad, A_WIDTH), F32),
                   jax.ShapeDtypeStruct((nb, A_HEADS, A_HEAD, A_HEAD), F32)],
        scratch_shapes=[pltpu.VMEM((A_HEADS, A_HEAD, A_HEAD), F32)],
        compiler_params=_cparams(("parallel", "arbitrary")),
        name="hgrn",
    )(*args)


def _hgrn_levels(c):
    out, size = [], SUBLANES
    while size < c:
        out.append(size)
        size *= 2
    return out


def _hgrn_chunk_kernel(aq_ref, af_ref, ai_ref, az_ref, lbl_ref, gn_ref, o_ref, sout_ref,
                       st_ref, q_s, k_s, b_s, oi_s, dg_s, qf_s, kf_s, mask_s, *, layer, c, t_valid):
    j = pl.program_id(1)
    levels = _hgrn_levels(c)
    ri = lax.broadcasted_iota(I32, (c, c), 0)
    ci = lax.broadcasted_iota(I32, (c, c), 1)

    @pl.when(j == 0)
    def _():
        for h in range(A_HEADS):
            st_ref[h] = jnp.zeros((A_HEAD, A_HEAD), F32)
        for n, size in enumerate(levels):
            same = (ri // (2 * size)) == (ci // (2 * size))
            mask_s[n] = jnp.where(same & ((ri % (2 * size)) >= size) & ((ci % (2 * size)) < size), 1.0, 0.0)

    lg = lbl_ref[...]
    e = jnp.exp(lg - jnp.max(lg, axis=0, keepdims=True))
    sm = e / jnp.sum(e, axis=0, keepdims=True)
    lb = jnp.zeros((1, A_WIDTH), F32)
    for i in range(1, layer + 1):
        lb = lb + sm[i:i + 1, :]

    valid = (j * c + lax.broadcasted_iota(I32, (c, 1), 0)) < t_valid
    f = af_ref[...]
    aq = aq_ref[...]
    logf = jnp.where(valid, jnp.log(lb + (1.0 - lb) * _sigmoid(f)), 0.0)
    kk = jnp.where(valid, (1.0 - lb) * _sigmoid(-f), 0.0)
    q = aq * _sigmoid(aq)
    b = jnp.dot((ri >= ci).astype(F32), logf, preferred_element_type=F32, precision=lax.Precision.HIGHEST)
    blast = b[c - 1:c, :]
    q_s[...] = q
    k_s[...] = kk
    b_s[...] = b
    for n, size in enumerate(levels):
        groups = c // (2 * size)
        bref = jnp.broadcast_to(b.reshape(groups, 2 * size, A_WIDTH)[:, size - 1:size, :], (groups, 2 * size, A_WIDTH))
        fac = jnp.exp(-jnp.abs(b - bref.reshape(c, A_WIDTH)))
        qf_s[n] = (q * fac).astype(BF16)
        kf_s[n] = (kk * fac).astype(BF16)
    qe = (q * jnp.exp(b)).astype(BF16)
    kd = (kk * jnp.exp(blast - b)).astype(BF16)
    dec = jnp.exp(blast)
    vb = ai_ref[...].astype(BF16)
    for h in range(A_HEADS):
        sl = slice(h * A_HEAD, (h + 1) * A_HEAD)
        st = st_ref[h]
        o = _nt(qe[:, sl], st.astype(BF16))
        if levels:
            att = jnp.zeros((c, c), F32)
            for n in range(len(levels)):
                att = att + _nt(qf_s[n, :, sl], kf_s[n, :, sl]) * mask_s[n]
            o = o + jnp.dot(att.astype(BF16), vb[:, sl], preferred_element_type=F32)
        oi_s[:, sl] = o
        st_ref[h] = st * dec[:, sl] + _tn(vb[:, sl], kd[:, sl])

    row8 = lax.broadcasted_iota(I32, (SUBLANES, 1), 0)
    for x in range(c // SUBLANES):
        rows = slice(x * SUBLANES, (x + 1) * SUBLANES)
        outs = []
        for h in range(A_HEADS):
            sl = slice(h * A_HEAD, (h + 1) * A_HEAD)
            qh, kh, bh, vh = q_s[rows, sl], k_s[rows, sl], b_s[rows, sl], ai_ref[rows, sl]
            o = jnp.zeros((SUBLANES, A_HEAD), F32)
            for s in range(SUBLANES):
                w = jnp.exp(bh - bh[s:s + 1, :])
                a = jnp.sum(qh * (kh[s:s + 1, :] * w), axis=-1, keepdims=True)
                o = o + jnp.where(row8 >= s, a, 0.0) * vh[s:s + 1, :]
            outs.append(o)
        dg_s[rows, :] = jnp.concatenate(outs, axis=1)

    gn = gn_ref[...]
    az = az_ref[...]
    outs = []
    for h in range(A_HEADS):
        sl = slice(h * A_HEAD, (h + 1) * A_HEAD)
        o = oi_s[:, sl] + dg_s[:, sl]
        outs.append(o * lax.rsqrt(jnp.mean(o * o, axis=-1, keepdims=True) + EPS) * gn)
    o_ref[...] = jnp.concatenate(outs, axis=1) * (az * _sigmoid(az))

    @pl.when(j == pl.num_programs(1) - 1)
    def _():
        for h in range(A_HEADS):
            sout_ref[h] = st_ref[h].T


def _hgrn_chunked(u, lb_logits, gn_g, *, layer, nb, t_pad, t_valid):
    c = LANES
    nj = t_pad // c
    nlev = max(len(_hgrn_levels(c)), 1)

    def ublk(blk):
        return pl.BlockSpec((c, 512), lambda b, j: (b * nj + j, blk))

    return pl.pallas_call(
        functools.partial(_hgrn_chunk_kernel, layer=layer, c=c, t_valid=t_valid),
        grid=(nb, nj),
        in_specs=[ublk(BLK_AQ), ublk(BLK_AF), ublk(BLK_AI), ublk(BLK_AZ),
                  pl.BlockSpec(lb_logits.shape, lambda b, j: (0, 0)),
                  pl.BlockSpec((1, A_HEAD), lambda b, j: (0, 0))],
        out_specs=[pl.BlockSpec((c, A_WIDTH), lambda b, j: (b * nj + j, 0)),
                   pl.BlockSpec((None, A_HEADS, A_HEAD, A_HEAD), lambda b, j: (b, 0, 0, 0))],
        out_shape=[jax.ShapeDtypeStruct((nb * t_pad, A_WIDTH), F32),
                   jax.ShapeDtypeStruct((nb, A_HEADS, A_HEAD, A_HEAD), F32)],
        scratch_shapes=[pltpu.VMEM((A_HEADS, A_HEAD, A_HEAD), F32),
                        pltpu.VMEM((c, A_WIDTH), F32),
                        pltpu.VMEM((c, A_WIDTH), F32),
                        pltpu.VMEM((c, A_WIDTH), F32),
                        pltpu.VMEM((c, A_WIDTH), F32),
                        pltpu.VMEM((c, A_WIDTH), F32),
                        pltpu.VMEM((nlev, c, A_WIDTH), BF16),
                        pltpu.VMEM((nlev, c, A_WIDTH), BF16),
                        pltpu.VMEM((nlev, c, c), F32)],
        compiler_params=_cparams(("parallel", "arbitrary")),
        name="hgrn_chunked",
    )(u, u, u, u, lb_logits, gn_g)


def _conv_tail(y, bz, lng, lnb, pw_ref):
    mu = jnp.mean(y, axis=-1, keepdims=True)
    d = y - mu
    var = jnp.mean(d * d, axis=-1, keepdims=True)
    y = d * lax.rsqrt(var + EPS) * lng + lnb
    y = y * _sigmoid(y)
    z = jnp.dot(y.astype(BF16), pw_ref[...], preferred_element_type=F32)
    return z * (bz * _sigmoid(bz))


def _conv_prompt_kernel(a_ref, g_ref, bz_ref, w_ref, cb_ref, lng_ref, lnb_ref, pw_ref, o_ref, st_ref, xb_ref, y_ref, xs_ref, *, rb, tail_off):
    j = pl.program_id(1)
    halo = 32

    @pl.when(j == 0)
    def _():
        xb_ref[0:halo, :] = jnp.zeros((halo, B_WIDTH), F32)

    xb_ref[halo:halo + rb, :] = a_ref[...] * _sigmoid(g_ref[...])
    span = rb + halo - SUBLANES
    for s in range(1, SUBLANES):
        xs_ref[s - 1, 0:span, :] = xb_ref[pl.ds(s, span), :]
    rc = rb // 2
    for c in range(B_WIDTH // LANES):
        cs = slice(c * LANES, (c + 1) * LANES)
        wc = w_ref[:, cs]
        for r in range(rb // rc):
            acc = jnp.zeros((rc, LANES), F32) + cb_ref[:, cs]
            for t in range(CONV_W):
                off = halo - (CONV_W - 1) + t
                s, base = off % SUBLANES, (off // SUBLANES) * SUBLANES + r * rc
                win = xb_ref[base:base + rc, cs] if s == 0 else xs_ref[s - 1, base:base + rc, cs]
                acc = acc + win * wc[t:t + 1, :]
            y_ref[r * rc:(r + 1) * rc, cs] = acc
    o_ref[...] = _conv_tail(y_ref[...], bz_ref[...], lng_ref[...], lnb_ref[...], pw_ref)

    @pl.when(j == pl.num_programs(1) - 1)
    def _():
        st_ref[...] = xb_ref[pl.ds(tail_off, CONV_W - 1), :]

    xb_ref[0:halo, :] = xb_ref[rb:rb + halo, :]


def _conv_prompt(u, cw, cb, lng, lnb, pw, *, nb, t_pad, t_valid):
    rb = max(t for t in range(2 * SUBLANES, 321, 2 * SUBLANES) if t_pad % t == 0)
    nj = t_pad // rb
    tail_off = (t_valid - (CONV_W - 1)) - ((nj - 1) * rb - 32)
    assert 0 <= tail_off and tail_off + CONV_W - 1 <= rb + 32

    def ublk(blk):
        return pl.BlockSpec((rb, 512), lambda b, j: (b * nj + j, blk))

    def full(a):
        return pl.BlockSpec(a.shape, lambda b, j: (0,) * a.ndim)

    return pl.pallas_call(
        functools.partial(_conv_prompt_kernel, rb=rb, tail_off=tail_off),
        grid=(nb, nj),
        in_specs=[ublk(BLK_GLUA), ublk(BLK_GLUG), ublk(BLK_BZ), full(cw), full(cb), full(lng), full(lnb), full(pw)],
        out_specs=[pl.BlockSpec((rb, B_WIDTH), lambda b, j: (b * nj + j, 0)),
                   pl.BlockSpec((None, CONV_W - 1, B_WIDTH), lambda b, j: (b, 0, 0))],
        out_shape=[jax.ShapeDtypeStruct((nb * t_pad, B_WIDTH), F32),
                   jax.ShapeDtypeStruct((nb, CONV_W - 1, B_WIDTH), F32)],
        scratch_shapes=[pltpu.VMEM((rb + 32, B_WIDTH), F32), pltpu.VMEM((rb, B_WIDTH), F32),
                        pltpu.VMEM((SUBLANES - 1, rb + 32, B_WIDTH), F32)],
        compiler_params=_cparams(("parallel", "arbitrary")),
        name="conv_prompt",
    )(u, u, u, cw, cb, lng, lnb, pw)


def _conv_decode_kernel(a_ref, g_ref, bz_ref, sin_ref, w_ref, cb_ref, lng_ref, lnb_ref, pw_ref, o_ref, sout_ref, xs_ref, *, ns, ds):
    nbuf = CONV_W - 1
    xs_ref[:, 0:nbuf, :] = sin_ref[...]
    glu = a_ref[...] * _sigmoid(g_ref[...])
    xs_ref[:, nbuf:nbuf + ds, :] = glu.reshape(ns, ds, B_WIDTH)
    acc = jnp.zeros((ns, ds, B_WIDTH), F32) + cb_ref[...]
    for t in range(CONV_W):
        acc = acc + xs_ref[:, pl.ds(t, ds), :] * w_ref[t:t + 1, :]
    o_ref[...] = _conv_tail(acc.reshape(ns * ds, B_WIDTH), bz_ref[...], lng_ref[...], lnb_ref[...], pw_ref)
    sout_ref[...] = xs_ref[:, pl.ds(ds, nbuf), :]


def _conv_decode(u, state, cw, cb, lng, lnb, pw, *, layer, nb, ds):
    ns = next(t for t in (16, 8, 4, 2, 1) if nb % t == 0)
    rb = ns * ds

    def ublk(blk):
        return pl.BlockSpec((rb, 512), lambda i: (i, blk))

    def full(a):
        return pl.BlockSpec(a.shape, lambda i: (0,) * a.ndim)

    return pl.pallas_call(
        functools.partial(_conv_decode_kernel, ns=ns, ds=ds),
        grid=(nb // ns,),
        in_specs=[ublk(BLK_GLUA), ublk(BLK_GLUG), ublk(BLK_BZ),
                  pl.BlockSpec((None, ns, CONV_W - 1, B_WIDTH), lambda i: (layer, i, 0, 0)),
                  full(cw), full(cb), full(lng), full(lnb), full(pw)],
        out_specs=[pl.BlockSpec((rb, B_WIDTH), lambda i: (i, 0)),
                   pl.BlockSpec((ns, CONV_W - 1, B_WIDTH), lambda i: (i, 0, 0))],
        out_shape=[jax.ShapeDtypeStruct((nb * ds, B_WIDTH), F32),
                   jax.ShapeDtypeStruct((nb, CONV_W - 1, B_WIDTH), F32)],
        scratch_shapes=[pltpu.VMEM((ns, CONV_W - 1 + ds, B_WIDTH), F32)],
        compiler_params=_cparams(("parallel",)),
        name="conv_decode",
    )(u, u, u, state, cw, cb, lng, lnb, pw)


def _dsa_tile_kernel(q_ref, qi_ref, cz_ref, kv_ref, kiw_ref, o_ref,
                     k16, vt16, ki16, wq_s, ik, hi16, lo16, s_buf, acc_s, *, n_sel, nblk, nb1, qw, t_pad):
    i = pl.program_id(1)
    blk = LANES
    hd = C_HEAD_DIM
    vregs = blk // SUBLANES
    qblocks = qw // blk
    groups = C_HEADS // C_KV_HEADS
    gw = groups * qw

    @pl.when(i == 0)
    def _():
        for kb in range(nblk):
            sl = slice(kb * blk, (kb + 1) * blk)
            x = kv_ref[sl, :]
            for hk in range(C_KV_HEADS):
                k16[hk * nb1 + kb] = x[:, hk * hd:(hk + 1) * hd].astype(BF16)
            vt16[kb] = x[:, blk:2 * blk].T.astype(BF16)
            ki16[kb] = kiw_ref[sl, :][:, 0:IDX_DIM].astype(BF16)
        for kb in range(nblk, nb1):
            for hk in range(C_KV_HEADS):
                k16[hk * nb1 + kb] = jnp.zeros((blk, hd), BF16)
            vt16[kb] = jnp.zeros((blk, blk), BF16)
            ki16[kb] = jnp.zeros((blk, IDX_DIM), BF16)
        wq_s[0:t_pad, :] = kiw_ref[...]
        if nb1 * blk > t_pad:
            wq_s[t_pad:nb1 * blk, :] = jnp.zeros((nb1 * blk - t_pad, blk), F32)

    npair = ((i + 1) * qblocks + 1) // 2
    r0 = pl.multiple_of(i * qw, qw)
    wq = wq_s[pl.ds(r0, qw), :]
    w_t = jnp.concatenate([wq[a * blk:(a + 1) * blk, :].T for a in range(qblocks)], axis=1)
    qi = qi_ref[...].astype(BF16)
    rowi = lax.broadcasted_iota(I32, (blk, qw), 0)
    qpos = r0 + lax.broadcasted_iota(I32, (blk, qw), 1)

    qis = jnp.concatenate([qi[:, h * IDX_DIM:(h + 1) * IDX_DIM] for h in range(IDX_HEADS)], axis=0)
    w_h = [w_t[IDX_DIM + h:IDX_DIM + h + 1, :] * IDX_SCALE for h in range(IDX_HEADS)]

    def p1(pp, c):
        for u in range(2):
            kb = 2 * pp + u
            s = _nt(ki16[kb], qis)
            acc = jnp.zeros((blk, qw), F32)
            for h in range(IDX_HEADS):
                acc = acc + jnp.maximum(s[:, h * qw:(h + 1) * qw], 0.0) * w_h[h]
            allowed = (kb * blk + rowi) <= qpos
            key = jnp.where(allowed, _sort_key(acc), jnp.int32(INT_MIN))
            ik[kb] = key
            hi16[kb] = (key >> 16).astype(I16)
        return c

    lax.fori_loop(0, npair, p1, 0)

    one16 = jnp.ones((), BF16)
    zero16 = jnp.zeros((), BF16)

    def count16(ref, pred):
        def body(pp, c):
            parts = []
            for u in range(2):
                m = jnp.where(pred(ref[2 * pp + u]), one16, zero16).reshape(vregs // 2, 2 * SUBLANES, qw)
                parts += [m[j] for j in range(vregs // 2)]
            while len(parts) > 1:
                parts = [parts[j] + parts[j + 1] for j in range(0, len(parts), 2)]
            return c + parts[0]
        c = lax.fori_loop(0, npair, body, jnp.zeros((2 * SUBLANES, qw), BF16))
        return jnp.sum(c.astype(F32), axis=0, keepdims=True)

    def search16(ref, want):
        lo = jnp.full((1, qw), -(1 << 15), I32)
        zero = jnp.zeros((1, qw), I32)
        v = jnp.where(count16(ref, lambda k: k >= zero.astype(I16)) >= want, zero, lo)

        def bit_body(t, v):
            cand = v + lax.shift_left(jnp.int32(1), jnp.int32(14) - t)
            return jnp.where(count16(ref, lambda k: k >= cand.astype(I16)) >= want, cand, v)

        return lax.fori_loop(0, 15, bit_body, v)

    thr_hi = search16(hi16, jnp.float32(n_sel))
    thr_hi16 = thr_hi.astype(I16)
    want_lo = n_sel - count16(hi16, lambda k: k > thr_hi16)

    def p2(pp, c):
        for u in range(2):
            kb = 2 * pp + u
            low = (ik[kb] & jnp.int32(0xFFFF)) - jnp.int32(1 << 15)
            lo16[kb] = jnp.where((ik[kb] >> 16) == thr_hi, low, jnp.int32(-(1 << 15))).astype(I16)
        return c

    lax.fori_loop(0, npair, p2, 0)
    thr_lo = search16(lo16, want_lo)
    thr = thr_hi * jnp.int32(1 << 16) + (thr_lo + jnp.int32(1 << 15))

    def count32(pred):
        def body(pp, c):
            for u in range(2):
                m = jnp.where(pred(ik[2 * pp + u]), 1, 0)
                c = c + jnp.sum(m.reshape(vregs, SUBLANES, qw), axis=0)
            return c
        c = lax.fori_loop(0, npair, body, jnp.zeros((SUBLANES, qw), I32))
        return jnp.sum(c, axis=0, keepdims=True)

    need = (n_sel - count32(lambda k: k > thr)).astype(F32)

    ltri = (lax.broadcasted_iota(I32, (blk, blk), 0) >= lax.broadcasted_iota(I32, (blk, blk), 1)).astype(BF16)
    q = (q_ref[...] * C_HEAD_DIM ** -0.5).astype(BF16)
    qg = [jnp.concatenate([q[:, (hk * groups + g) * hd:(hk * groups + g + 1) * hd] for g in range(groups)], axis=0)
          for hk in range(C_KV_HEADS)]

    def p3(pp, carry):
        tcar = carry[0]
        mx = list(carry[1:])
        for u in range(2):
            kb = 2 * pp + u
            key = ik[kb]
            tie = key == thr
            pre = jnp.dot(ltri, jnp.where(tie, 1.0, 0.0).astype(BF16), preferred_element_type=F32) + tcar
            allowed = (kb * blk + rowi) <= qpos
            sel = ((tie & (pre <= need)) | (key > thr)) & allowed
            bias = jnp.where(sel, 0.0, MASK_NEG)
            tcar = pre[blk - 1:blk, :]
            bias_g = jnp.concatenate([bias] * groups, axis=1)
            for hk in range(C_KV_HEADS):
                s = _nt(k16[hk * nb1 + kb], qg[hk]) + bias_g
                s_buf[hk * nb1 + kb] = s
                mx[hk] = jnp.maximum(mx[hk], jnp.max(s.reshape(vregs, SUBLANES, gw), axis=0))
        return (tcar, *mx)

    init = (jnp.zeros((1, qw), F32),) + tuple(jnp.full((SUBLANES, gw), -3.0e38, F32) for _ in range(C_KV_HEADS))
    res = lax.fori_loop(0, npair, p3, init)
    m = [jnp.max(res[1 + hk], axis=0, keepdims=True) for hk in range(C_KV_HEADS)]

    acc_s[...] = jnp.zeros((C_KV_HEADS, hd, gw), F32)

    def p4(pp, carry):
        ls = list(carry)
        for u in range(2):
            kb = 2 * pp + u
            vt = vt16[kb]
            for hk in range(C_KV_HEADS):
                p = jnp.exp(s_buf[hk * nb1 + kb] - m[hk])
                ls[hk] = ls[hk] + jnp.sum(p.reshape(vregs, SUBLANES, gw), axis=0)
                acc_s[hk] = acc_s[hk] + jnp.dot(vt[hk * hd:(hk + 1) * hd, :], p.astype(BF16), preferred_element_type=F32)
        return tuple(ls)

    ls = lax.fori_loop(0, npair, p4, tuple(jnp.zeros((SUBLANES, gw), F32) for _ in range(C_KV_HEADS)))
    outs = []
    for hk in range(C_KV_HEADS):
        og = acc_s[hk] / jnp.sum(ls[hk], axis=0, keepdims=True)
        outs += [og[:, g * qw:(g + 1) * qw] for g in range(groups)]
    o_t = jnp.concatenate(outs, axis=0)
    o = jnp.concatenate(
        [jnp.concatenate([o_t[j * blk:(j + 1) * blk, a * blk:(a + 1) * blk].T for j in range(C_WIDTH // blk)], axis=1)
         for a in range(qblocks)], axis=0)
    cz = cz_ref[...]
    o_ref[...] = o * (cz * _sigmoid(cz))


def _dsa_tiles(u, *, nb, t_pad, n_sel):
    blk = LANES
    qw = 2 * blk
    nblk = t_pad // blk
    nq = -(-t_pad // qw)
    nb1 = nq * (qw // blk)
    gw = (C_HEADS // C_KV_HEADS) * qw
    u3 = u.reshape(nb, t_pad, U_W)
    out = pl.pallas_call(
        functools.partial(_dsa_tile_kernel, n_sel=n_sel, nblk=nblk, nb1=nb1, qw=qw, t_pad=t_pad),
        grid=(nb, nq),
        in_specs=[
            pl.BlockSpec((None, qw, 512), lambda b, i: (b, i, BLK_CQ)),
            pl.BlockSpec((None, qw, 256), lambda b, i: (b, i, BLK256_CQI)),
            pl.BlockSpec((None, qw, 512), lambda b, i: (b, i, BLK_CZ)),
            pl.BlockSpec((None, t_pad, 256), lambda b, i: (b, 0, BLK256_KV)),
            pl.BlockSpec((None, t_pad, LANES), lambda b, i: (b, 0, BLK_KIW)),
        ],
        out_specs=pl.BlockSpec((None, qw, C_WIDTH), lambda b, i: (b, i, 0)),
        out_shape=jax.ShapeDtypeStruct((nb, t_pad, C_WIDTH), F32),
        scratch_shapes=[
            pltpu.VMEM((C_KV_HEADS * nb1, blk, C_HEAD_DIM), BF16),
            pltpu.VMEM((nb1, blk, blk), BF16),
            pltpu.VMEM((nb1, blk, IDX_DIM), BF16),
            pltpu.VMEM((nb1 * blk, LANES), F32),
            pltpu.VMEM((nb1, blk, qw), I32),
            pltpu.VMEM((nb1, blk, qw), I16),
            pltpu.VMEM((nb1, blk, qw), I16),
            pltpu.VMEM((C_KV_HEADS * nb1, blk, gw), F32),
            pltpu.VMEM((C_KV_HEADS, C_HEAD_DIM, gw), F32),
        ],
        compiler_params=_cparams(("parallel", "arbitrary")),
        name="dsa_tiles",
    )(u3, u3, u3, u3, u3)
    return out.reshape(nb * t_pad, C_WIDTH)


def _dsa_decode_kernel(pt_ref, q_ref, qi_ref, cz_ref, kvn_ref, kiwn_ref, ck_hbm, cv_hbm, ci_hbm, o_ref,
                       kcat, vcat, icat, kbuf, vbuf, ibuf, sems, *, layer, n_sel, npages, ds):
    b = pl.program_id(0)
    slot = b % 2
    sources = ((ck_hbm, kbuf), (cv_hbm, vbuf), (ci_hbm, ibuf))

    def page_copy(kind, page, to_slot, j):
        src, buf = sources[kind]
        return pltpu.make_async_copy(src.at[layer, page], buf.at[to_slot, j], sems.at[to_slot, kind])

    def fetch(seq, to_slot):
        for j in range(npages):
            page = pt_ref[seq * npages + j]
            for kind in range(len(sources)):
                page_copy(kind, page, to_slot, j).start()

    @pl.when(b == 0)
    def _():
        fetch(0, 0)

    for j in range(npages):
        for kind in range(len(sources)):
            page_copy(kind, 0, slot, j).wait()

    @pl.when(b + 1 < pl.num_programs(0))
    def _():
        fetch(b + 1, 1 - slot)

    kpages = [kbuf.at[slot, j] for j in range(npages)]
    vpages = [vbuf.at[slot, j] for j in range(npages)]
    ipages = [ibuf.at[slot, j] for j in range(npages)]
    q, qi, cz, kvn, kiwn = q_ref[...] * C_HEAD_DIM ** -0.5, qi_ref[...], cz_ref[...], kvn_ref[...], kiwn_ref[...]
    blk = LANES
    hd = C_HEAD_DIM
    nblk = npages + 1
    nkeys = nblk * blk
    groups = C_HEADS // C_KV_HEADS

    zpad = jnp.zeros((blk - ds, blk), F32)
    for j in range(npages):
        sl = slice(j * blk, (j + 1) * blk)
        kcat[:, sl] = kpages[j][...].astype(BF16)
        vcat[:, sl] = vpages[j][...].astype(BF16)
        icat[:, sl] = ipages[j][...].astype(BF16)
    new = slice(npages * blk, nkeys)
    kcat[:, new] = jnp.concatenate([kvn[:, 0:blk], zpad], axis=0).T.astype(BF16)
    vcat[:, new] = jnp.concatenate([kvn[:, blk:2 * blk], zpad], axis=0).T.astype(BF16)
    icat[:, new] = jnp.concatenate([kiwn, zpad], axis=0).T[0:IDX_DIM, :].astype(BF16)
    rowq = lax.broadcasted_iota(I32, (ds, nkeys), 0)
    kpos = lax.broadcasted_iota(I32, (ds, nkeys), 1)
    allowed = kpos <= npages * blk + rowq

    qis = jnp.concatenate([qi[:, h * IDX_DIM:(h + 1) * IDX_DIM] for h in range(IDX_HEADS)], axis=0).astype(BF16)
    wcol = jnp.concatenate([kiwn[:, IDX_DIM + h:IDX_DIM + h + 1] for h in range(IDX_HEADS)], axis=0) * IDX_SCALE
    r = jnp.maximum(jnp.dot(qis, icat[...], preferred_element_type=F32), 0.0) * wcol
    acc = r[0:ds]
    for h in range(1, IDX_HEADS):
        acc = acc + r[h * ds:(h + 1) * ds]
    keys = jnp.where(allowed, _sort_key(jnp.zeros((ds, nkeys), F32) + acc), jnp.int32(INT_MIN))

    def count_ge(cand):
        return jnp.sum(jnp.where(keys >= cand, 1, 0), axis=1, keepdims=True)

    thr = jnp.full((ds, 1), INT_MIN, I32)
    for step in range(8):
        shift = 28 - 4 * step
        digit = jnp.zeros((ds, 1), I32)
        for c in range(1, 16):
            inc = int(np.array(c << shift, dtype=np.uint32).astype(np.int32))
            digit = digit + jnp.where(count_ge(thr + jnp.int32(inc)) >= n_sel, 1, 0)
        thr = thr + digit * jnp.int32(1 << shift)

    gt = keys > thr
    tie = keys == thr
    need = (n_sel - jnp.sum(jnp.where(gt, 1, 0), axis=1, keepdims=True)).astype(F32)
    tief = jnp.where(tie, 1.0, 0.0)
    ri = lax.broadcasted_iota(I32, (blk, blk), 0)
    ci = lax.broadcasted_iota(I32, (blk, blk), 1)
    utri = (ri <= ci).astype(F32)
    carry = jnp.zeros((ds, 1), F32)
    biases = []
    for j in range(nblk):
        sl = slice(j * blk, (j + 1) * blk)
        pre = jnp.dot(tief[:, sl], utri, preferred_element_type=F32) + carry
        sel = ((tie[:, sl] & (pre <= need)) | gt[:, sl]) & allowed[:, sl]
        biases.append(jnp.where(sel, 0.0, MASK_NEG))
        carry = carry + jnp.sum(tief[:, sl], axis=1, keepdims=True)
    bias = jnp.concatenate(biases, axis=1)
    bias_all = jnp.concatenate([bias] * C_HEADS, axis=0)

    zq = jnp.zeros((groups * ds, hd), F32)
    qrows = []
    for hk in range(C_KV_HEADS):
        qg = jnp.concatenate([q[:, (hk * groups + g) * hd:(hk * groups + g + 1) * hd] for g in range(groups)], axis=0)
        qrows.append(jnp.concatenate([qg if c == hk else zq for c in range(C_KV_HEADS)], axis=1))
    qd = jnp.concatenate(qrows, axis=0).astype(BF16)
    s = jnp.dot(qd, kcat[...], preferred_element_type=F32) + bias_all
    m = jnp.max(s, axis=1, keepdims=True)
    p = jnp.exp(s - m)
    l = jnp.sum(p, axis=1, keepdims=True)
    og = _nt(p.astype(BF16), vcat[...]) / l
    outs = []
    for hk in range(C_KV_HEADS):
        for g in range(groups):
            r0 = (hk * groups + g) * ds
            outs.append(og[r0:r0 + ds, hk * hd:(hk + 1) * hd])
    o = jnp.concatenate(outs, axis=1)
    o_ref[...] = o * (cz * _sigmoid(cz))


def _dsa_decode(u, cache_k, cache_v, cache_kidx, page_table, *, layer, nb, ds, n_sel):
    npages = page_table.shape[1]
    pt = page_table.reshape(-1)
    kvw = C_KV_HEADS * C_HEAD_DIM
    hbm = pl.BlockSpec(memory_space=pl.ANY)
    grid_spec = pltpu.PrefetchScalarGridSpec(
        num_scalar_prefetch=1,
        grid=(nb,),
        in_specs=[
            pl.BlockSpec((ds, 512), lambda b, pt_ref: (b, BLK_CQ)),
            pl.BlockSpec((ds, 256), lambda b, pt_ref: (b, BLK256_CQI)),
            pl.BlockSpec((ds, 512), lambda b, pt_ref: (b, BLK_CZ)),
            pl.BlockSpec((ds, 256), lambda b, pt_ref: (b, BLK256_KV)),
            pl.BlockSpec((ds, LANES), lambda b, pt_ref: (b, BLK_KIW)),
            hbm, hbm, hbm,
        ],
        out_specs=pl.BlockSpec((ds, C_WIDTH), lambda b, pt_ref: (b, 0)),
        scratch_shapes=[pltpu.VMEM((kvw, (npages + 1) * LANES), BF16),
                        pltpu.VMEM((kvw, (npages + 1) * LANES), BF16),
                        pltpu.VMEM((IDX_DIM, (npages + 1) * LANES), BF16),
                        pltpu.VMEM((2, npages, kvw, PAGE_SIZE), F32),
                        pltpu.VMEM((2, npages, kvw, PAGE_SIZE), F32),
                        pltpu.VMEM((2, npages, IDX_DIM, PAGE_SIZE), F32),
                        pltpu.SemaphoreType.DMA((2, 3))],
    )
    return pl.pallas_call(
        functools.partial(_dsa_decode_kernel, layer=layer, n_sel=n_sel, npages=npages, ds=ds),
        grid_spec=grid_spec,
        out_shape=jax.ShapeDtypeStruct((nb * ds, C_WIDTH), F32),
        compiler_params=_cparams(("arbitrary",)),
        name="dsa_decode",
    )(pt, u, u, u, u, u, cache_k, cache_v, cache_kidx)


def _merge_kernel(h_ref, ya_ref, yb_ref, yc_ref, ga_ref, gb_ref, gc_ref, wpa_ref, wpb_ref, wpc_ref, wo_ref, fg_ref, o_ref, *, final):
    def proj(y_ref, w_ref):
        return jnp.dot(y_ref[...].astype(BF16), w_ref[...], preferred_element_type=F32)

    m = (_sigmoid(ga_ref[...]) * proj(ya_ref, wpa_ref)
         + _sigmoid(gb_ref[...]) * proj(yb_ref, wpb_ref)
         + _sigmoid(gc_ref[...]) * proj(yc_ref, wpc_ref))
    h = h_ref[...] + jnp.dot(m.astype(BF16), wo_ref[...], preferred_element_type=F32)
    if final:
        h = h * lax.rsqrt(jnp.mean(h * h, axis=-1, keepdims=True) + EPS) * fg_ref[...]
    o_ref[...] = h


def _merge(h, ya, yb, yc, u, wpa, wpb, wpc, wo, fg, *, final):
    rows = h.shape[0]
    tm = next(t for t in (512, 256, 128, 64, 32, 16, 8) if rows % t == 0)

    def rowblk(width, blk=0):
        return pl.BlockSpec((tm, width), lambda i: (i, blk))

    def full(a):
        return pl.BlockSpec(a.shape, lambda i: (0,) * a.ndim)

    return pl.pallas_call(
        functools.partial(_merge_kernel, final=final),
        grid=(rows // tm,),
        in_specs=[rowblk(D_MODEL), rowblk(512), rowblk(512), rowblk(512),
                  rowblk(D_MODEL, BLK_GATE), rowblk(D_MODEL, BLK_GATE + 1), rowblk(D_MODEL, BLK_GATE + 2),
                  full(wpa), full(wpb), full(wpc), full(wo), full(fg)],
        out_specs=rowblk(D_MODEL),
        out_shape=jax.ShapeDtypeStruct((rows, D_MODEL), F32),
        compiler_params=_cparams(("parallel",)),
        name="merge",
    )(h, ya, yb, yc, u, u, u, wpa, wpb, wpc, wo, fg)


def _rope_tables(pos):
    half = ROT_DIM // 2
    inv = ROPE_THETA ** (-jnp.arange(half, dtype=F32) * 2.0 / ROT_DIM)
    ang = pos.astype(F32)[:, None] * jnp.tile(inv, LANES // half)[None, :]
    return jnp.cos(ang), jnp.sin(ang)


def _cache_rows(u, nb, t_pad, t_valid):
    u3 = u.reshape(nb, t_pad, U_W)[:, :t_valid]
    k0, ki0 = BLK256_KV * 256, BLK_KIW * LANES
    heads = (nb, t_valid, C_KV_HEADS, C_HEAD_DIM)
    return (u3[..., k0:k0 + LANES].reshape(heads), u3[..., k0 + LANES:k0 + 2 * LANES].reshape(heads),
            u3[..., ki0:ki0 + IDX_DIM])


def _pack_cols(w):
    split = 4608
    tail = split + 68
    pad = jnp.zeros(w.shape[:-1] + (U_W - N_IN,), w.dtype)
    return jnp.concatenate([w[..., :split], w[..., tail:], w[..., split:tail], pad], axis=-1)


def kernel(x_prompt, x_sample, cache_k, cache_v, cache_kidx, state_hgrn, state_conv, page_table, meta_tokens, norm_g, w_in, b_in, lb_logits, hgrn_norm_g, conv_w, conv_b, conv_ln_g, conv_ln_b, conv_pw, w_pa, w_pb, w_pc, w_out, final_norm_g):
    nbp, seq, _ = x_prompt.shape
    nbs, ds, _ = x_sample.shape
    depth = w_in.shape[0]
    npages = page_table.shape[1]
    past = npages * PAGE_SIZE
    t_valid = seq + N_META
    t_pad = -(-t_valid // LANES) * LANES
    n_sel_p = min(TOPK_MAX, t_valid // 4)
    n_sel_s = min(TOPK_MAX, (past + ds) // 4)
    n_phys = cache_k.shape[1]

    meta = jnp.broadcast_to(meta_tokens[None].astype(F32), (nbp, N_META, D_MODEL))
    hp = jnp.concatenate([meta, x_prompt, jnp.zeros((nbp, t_pad - t_valid, D_MODEL), F32)], axis=1).reshape(nbp * t_pad, D_MODEL)
    hs = x_sample.reshape(nbs * ds, D_MODEL)
    cos_p, sin_p = _rope_tables(jnp.tile(jnp.arange(t_pad), nbp))
    cos_s, sin_s = _rope_tables(jnp.tile(past + jnp.arange(ds), nbs))

    w_in_p = _pack_cols(w_in).astype(BF16).reshape(depth, D_MODEL, U_W // TN_IN, TN_IN).transpose(0, 2, 1, 3)
    b_in_p = _pack_cols(b_in).reshape(depth, 1, U_W)
    ck4 = cache_k.transpose(0, 1, 3, 4, 2).reshape(depth, n_phys, C_KV_HEADS * C_HEAD_DIM, PAGE_SIZE)
    cv4 = cache_v.transpose(0, 1, 3, 4, 2).reshape(depth, n_phys, C_KV_HEADS * C_HEAD_DIM, PAGE_SIZE)
    ci4 = cache_kidx.transpose(0, 1, 3, 2)
    fg = final_norm_g.reshape(1, D_MODEL)

    outs = {k: [] for k in ("pk", "pv", "pki", "ps", "pc", "sk", "sv", "ski", "ss", "sc")}
    for l in range(depth):
        g = norm_g[l].reshape(1, D_MODEL)
        gn = hgrn_norm_g[l].reshape(1, A_HEAD)
        cw, cb = conv_w[l], conv_b[l].reshape(1, B_WIDTH)
        lng, lnb = conv_ln_g[l].reshape(1, B_WIDTH), conv_ln_b[l].reshape(1, B_WIDTH)
        pw = conv_pw[l].astype(BF16)
        wpa, wpb, wpc, wo = (w[l].astype(BF16) for w in (w_pa, w_pb, w_pc, w_out))
        final = l == depth - 1

        u = _inproj(hp, g, w_in_p[l], b_in_p[l], cos_p, sin_p)
        ya, s_new = _hgrn_chunked(u, lb_logits, gn, layer=l, nb=nbp, t_pad=t_pad, t_valid=t_valid)
        yb, c_new = _conv_prompt(u, cw, cb, lng, lnb, pw, nb=nbp, t_pad=t_pad, t_valid=t_valid)
        yc = _dsa_tiles(u, nb=nbp, t_pad=t_pad, n_sel=n_sel_p)
        hp = _merge(hp, ya, yb, yc, u, wpa, wpb, wpc, wo, fg, final=final)
        k_new, v_new, ki_new = _cache_rows(u, nbp, t_pad, t_valid)
        for name, val in zip(("pk", "pv", "pki", "ps", "pc"), (k_new, v_new, ki_new, s_new, c_new)):
            outs[name].append(val)

        u = _inproj(hs, g, w_in_p[l], b_in_p[l], cos_s, sin_s)
        ya, s_new = _hgrn(u, lb_logits, gn, state_hgrn, layer=l, nb=nbs, t_pad=ds, t_valid=ds, rb=ds, sb=ds, state_layer=l)
        yb, c_new = _conv_decode(u, state_conv, cw, cb, lng, lnb, pw, layer=l, nb=nbs, ds=ds)
        yc = _dsa_decode(u, ck4, cv4, ci4, page_table, layer=l, nb=nbs, ds=ds, n_sel=n_sel_s)
        hs = _merge(hs, ya, yb, yc, u, wpa, wpb, wpc, wo, fg, final=final)
        k_new, v_new, ki_new = _cache_rows(u, nbs, ds, ds)
        for name, val in zip(("sk", "sv", "ski", "ss", "sc"), (k_new, v_new, ki_new, s_new, c_new)):
            outs[name].append(val)

    y_prompt = hp.reshape(nbp, t_pad, D_MODEL)[:, N_META:t_valid]
    y_sample = hs.reshape(nbs, ds, D_MODEL)
    st = {k: jnp.stack(v) for k, v in outs.items()}
    return (y_prompt, y_sample, st["pk"], st["pv"], st["pki"], st["ps"], st["pc"],
            st["sk"], st["sv"], st["ski"], st["ss"], st["sc"])
```

```python
import functools

import numpy as np
import jax
import jax.numpy as jnp
from jax import lax
from jax.experimental import pallas as pl
from jax.experimental.pallas import tpu as pltpu

F32 = jnp.float32
BF16 = jnp.bfloat16
I32 = jnp.int32
I16 = jnp.int16

D_MODEL = 1024
N_META = 16
EPS = 1e-6
MASK_NEG = -1e30
A_WIDTH = 512
A_HEAD = 128
A_HEADS = 4
B_WIDTH = 512
CONV_W = 31
C_HEADS = 8
C_HEAD_DIM = 64
C_WIDTH = 512
C_KV_HEADS = 2
IDX_HEADS = 4
IDX_DIM = 64
TOPK_MAX = 256
PAGE_SIZE = 128
ROPE_THETA = 500000.0
ROT_DIM = 16
IDX_SCALE = (IDX_HEADS * IDX_DIM) ** -0.5
INT_MIN = -(2 ** 31)

LANES = 128
SUBLANES = 8
VMEM_LIMIT = 48 * 1024 * 1024
VMEM_TILE_BUDGET = 28 * 1024 * 1024

N_IN = 8260
U_W = 8448
TN_IN = 768
BLK_AQ, BLK_AF, BLK_AI, BLK_AZ, BLK_GLUA, BLK_GLUG, BLK_BZ, BLK_CQ, BLK_KVQI, BLK_CZ = range(10)
BLK_GATE = 5
BLK_KIW = 64
BLK256_KV = 16
BLK256_CQI = 17


def _cparams(sem):
    return pltpu.CompilerParams(dimension_semantics=sem, vmem_limit_bytes=VMEM_LIMIT)


def _sigmoid(x):
    return 1.0 / (1.0 + jnp.exp(-x))


def _nt(a, b):
    return lax.dot_general(a, b, (((1,), (1,)), ((), ())), preferred_element_type=F32)


def _tn(a, b):
    return lax.dot_general(a, b, (((0,), (0,)), ((), ())), preferred_element_type=F32)


def _sort_key(x):
    bits = pltpu.bitcast(x, I32)
    return bits ^ ((bits >> 31) & jnp.int32(0x7FFFFFFF))


ROPE_TWO_HEADS = (28, 29, 30, 31, 32, 34, 35)
ROPE_ONE_HEAD = (BLK_KIW,)


def _inproj_kernel(x_ref, g_ref, w_ref, b_ref, cos_ref, sin_ref, o_ref, xn_ref):
    j = pl.program_id(1)

    @pl.when(j == 0)
    def _():
        x = x_ref[...]
        ms = jnp.mean(x * x, axis=-1, keepdims=True)
        xn_ref[...] = (x * lax.rsqrt(ms + EPS) * g_ref[...]).astype(BF16)

    o_ref[...] = jnp.dot(xn_ref[...], w_ref[...].astype(BF16), preferred_element_type=F32) + b_ref[...]

    per_tile = TN_IN // LANES
    half = ROT_DIM // 2
    lane = lax.broadcasted_iota(I32, (1, LANES), 1)
    for tile in sorted({blk // per_tile for blk in ROPE_TWO_HEADS + ROPE_ONE_HEAD}):
        @pl.when(j == tile)
        def _(tile=tile):
            cosd = cos_ref[...]
            sind = sin_ref[...]
            for blk in ROPE_TWO_HEADS + ROPE_ONE_HEAD:
                if blk // per_tile != tile:
                    continue
                pos = lane % C_HEAD_DIM if blk in ROPE_TWO_HEADS else lane
                c = jnp.where(pos < ROT_DIM, cosd, 1.0)
                a = jnp.where((pos >= half) & (pos < ROT_DIM), sind, 0.0)
                b = jnp.where(pos < half, -sind, 0.0)
                sl = slice((blk % per_tile) * LANES, (blk % per_tile + 1) * LANES)
                x = o_ref[:, sl]
                o_ref[:, sl] = x * c + pltpu.roll(x, half, 1) * a + pltpu.roll(x, LANES - half, 1) * b


def _inproj_row_tile(rows):
    per_row = 2 * D_MODEL * 4 + 2 * TN_IN * 4 + D_MODEL * 2 + 4 * LANES * 4
    fixed = 2 * (D_MODEL * TN_IN * 4 + TN_IN * 4 + D_MODEL * 4)
    return max(t for t in range(SUBLANES, rows + 1, SUBLANES) if rows % t == 0 and t * per_row + fixed <= VMEM_TILE_BUDGET)


def _inproj(h, g, w, b, cosd, sind):
    rows = h.shape[0]
    tm = _inproj_row_tile(rows)
    return pl.pallas_call(
        _inproj_kernel,
        grid=(rows // tm, U_W // TN_IN),
        in_specs=[
            pl.BlockSpec((tm, D_MODEL), lambda i, j: (i, 0)),
            pl.BlockSpec((1, D_MODEL), lambda i, j: (0, 0)),
            pl.BlockSpec((D_MODEL, TN_IN), lambda i, j: (0, j)),
            pl.BlockSpec((1, TN_IN), lambda i, j: (0, j)),
            pl.BlockSpec((tm, LANES), lambda i, j: (i, 0)),
            pl.BlockSpec((tm, LANES), lambda i, j: (i, 0)),
        ],
        out_specs=pl.BlockSpec((tm, TN_IN), lambda i, j: (i, j)),
        out_shape=jax.ShapeDtypeStruct((rows, U_W), F32),
        scratch_shapes=[pltpu.VMEM((tm, D_MODEL), BF16)],
        compiler_params=_cparams(("parallel", "arbitrary")),
        name="inproj",
    )(h, g, w, b, cosd, sind)


def _hgrn_kernel(*refs, layer, sb, rb, t_valid, has_s0):
    aq_ref, af_ref, ai_ref, az_ref, lbl_ref, gn_ref = refs[:6]
    if has_s0:
        s0_ref, o_ref, sout_ref, st_ref = refs[6:]
    else:
        o_ref, sout_ref, st_ref = refs[6:]
    j = pl.program_id(1)
    nsub_max = rb // sb

    @pl.when(j == 0)
    def _():
        for h in range(A_HEADS):
            if has_s0:
                st_ref[h] = s0_ref[h].T
            else:
                st_ref[h] = jnp.zeros((A_HEAD, A_HEAD), F32)

    lg = lbl_ref[...]
    e = jnp.exp(lg - jnp.max(lg, axis=0, keepdims=True))
    sm = e / jnp.sum(e, axis=0, keepdims=True)
    lb = jnp.zeros((1, A_WIDTH), F32)
    for i in range(1, layer + 1):
        lb = lb + sm[i:i + 1, :]

    n_sub = jnp.clip((t_valid - j * rb) // sb, 0, nsub_max)
    if nsub_max > 1:
        o_ref[...] = jnp.zeros((rb, A_WIDTH), F32)
    ri = lax.broadcasted_iota(I32, (sb, sb), 0)
    ci = lax.broadcasted_iota(I32, (sb, sb), 1)
    tri = (ri >= ci).astype(F32)
    rowi = lax.broadcasted_iota(I32, (sb, 1), 0)
    gn = gn_ref[...]

    def sub(i, carry):
        r0 = pl.multiple_of(i * sb, sb)
        f = af_ref[pl.ds(r0, sb), :]
        aq = aq_ref[pl.ds(r0, sb), :]
        v = ai_ref[pl.ds(r0, sb), :]
        az = az_ref[pl.ds(r0, sb), :]
        logf = jnp.log(lb + (1.0 - lb) * _sigmoid(f))
        kk = (1.0 - lb) * _sigmoid(-f)
        q = aq * _sigmoid(aq)
        b = jnp.dot(tri, logf, preferred_element_type=F32, precision=lax.Precision.HIGHEST)
        blast = b[sb - 1:sb, :]
        qe = q * jnp.exp(b)
        kd = kk * jnp.exp(blast - b)
        dec = jnp.exp(blast)
        outs = []
        for h in range(A_HEADS):
            sl = slice(h * A_HEAD, (h + 1) * A_HEAD)
            st = st_ref[h]
            o = _nt(qe[:, sl].astype(BF16), st.astype(BF16))
            qh, kh, vh, bh = q[:, sl], kk[:, sl], v[:, sl], b[:, sl]
            for s in range(sb):
                w = jnp.exp(bh - bh[s:s + 1, :])
                a = jnp.sum(qh * (kh[s:s + 1, :] * w), axis=-1, keepdims=True)
                o = o + jnp.where(rowi >= s, a, 0.0) * vh[s:s + 1, :]
            st_ref[h] = st * dec[:, sl] + _tn(vh.astype(BF16), kd[:, sl].astype(BF16))
            on = o * lax.rsqrt(jnp.mean(o * o, axis=-1, keepdims=True) + EPS) * gn
            outs.append(on)
        o_ref[pl.ds(r0, sb), :] = jnp.concatenate(outs, axis=1) * (az * _sigmoid(az))
        return carry

    lax.fori_loop(0, n_sub, sub, 0)

    @pl.when(j == pl.num_programs(1) - 1)
    def _():
        for h in range(A_HEADS):
            sout_ref[h] = st_ref[h].T


def _hgrn(u, lb_logits, gn_g, s0, *, layer, nb, t_pad, t_valid, rb, sb, state_layer=None):
    nj = t_pad // rb
    has_s0 = s0 is not None

    def ublk(blk):
        return pl.BlockSpec((rb, 512), lambda b, j: (b * nj + j, blk))

    in_specs = [ublk(BLK_AQ), ublk(BLK_AF), ublk(BLK_AI), ublk(BLK_AZ),
                pl.BlockSpec(lb_logits.shape, lambda b, j: (0, 0)),
                pl.BlockSpec((1, A_HEAD), lambda b, j: (0, 0))]
    args = [u, u, u, u, lb_logits, gn_g]
    if has_s0:
        in_specs.append(pl.BlockSpec((None, None, A_HEADS, A_HEAD, A_HEAD), lambda b, j: (state_layer, b, 0, 0, 0)))
        args.append(s0)
    return pl.pallas_call(
        functools.partial(_hgrn_kernel, layer=layer, sb=sb, rb=rb, t_valid=t_valid, has_s0=has_s0),
        grid=(nb, nj),
        in_specs=in_specs,
        out_specs=[pl.BlockSpec((rb, A_WIDTH), lambda b, j: (b * nj + j, 0)),
                   pl.BlockSpec((None, A_HEADS, A_HEAD, A_HEAD), lambda b, j: (b, 0, 0, 0))],
        out_shape=[jax.ShapeDtypeStruct((nb * t_pad, A_WIDTH), F32),
                   jax.ShapeDtypeStruct((nb, A_HEADS, A_HEAD, A_HEAD), F32)],
        scratch_shapes=[pltpu.VMEM((A_HEADS, A_HEAD, A_HEAD), F32)],
        compiler_params=_cparams(("parallel", "arbitrary")),
        name="hgrn",
    )(*args)


def _hgrn_levels(c):
    out, size = [], SUBLANES
    while size < c:
        out.append(size)
        size *= 2
    return out


def _hgrn_chunk_kernel(aq_ref, af_ref, ai_ref, az_ref, lbl_ref, gn_ref, o_ref, sout_ref,
                       st_ref, q_s, k_s, b_s, oi_s, dg_s, qf_s, kf_s, mask_s, *, layer, c, t_valid):
    j = pl.program_id(1)
    levels = _hgrn_levels(c)
    ri = lax.broadcasted_iota(I32, (c, c), 0)
    ci = lax.broadcasted_iota(I32, (c, c), 1)

    @pl.when(j == 0)
    def _():
        for h in range(A_HEADS):
            st_ref[h] = jnp.zeros((A_HEAD, A_HEAD), F32)
        for n, size in enumerate(levels):
            same = (ri // (2 * size)) == (ci // (2 * size))
            mask_s[n] = jnp.where(same & ((ri % (2 * size)) >= size) & ((ci % (2 * size)) < size), 1.0, 0.0)

    lg = lbl_ref[...]
    e = jnp.exp(lg - jnp.max(lg, axis=0, keepdims=True))
    sm = e / jnp.sum(e, axis=0, keepdims=True)
    lb = jnp.zeros((1, A_WIDTH), F32)
    for i in range(1, layer + 1):
        lb = lb + sm[i:i + 1, :]

    valid = (j * c + lax.broadcasted_iota(I32, (c, 1), 0)) < t_valid
    f = af_ref[...]
    aq = aq_ref[...]
    logf = jnp.where(valid, jnp.log(lb + (1.0 - lb) * _sigmoid(f)), 0.0)
    kk = jnp.where(valid, (1.0 - lb) * _sigmoid(-f), 0.0)
    q = aq * _sigmoid(aq)
    b = jnp.dot((ri >= ci).astype(F32), logf, preferred_element_type=F32, precision=lax.Precision.HIGHEST)
    blast = b[c - 1:c, :]
    q_s[...] = q
    k_s[...] = kk
    b_s[...] = b
    for n, size in enumerate(levels):
        groups = c // (2 * size)
        bref = jnp.broadcast_to(b.reshape(groups, 2 * size, A_WIDTH)[:, size - 1:size, :], (groups, 2 * size, A_WIDTH))
        fac = jnp.exp(-jnp.abs(b - bref.reshape(c, A_WIDTH)))
        qf_s[n] = (q * fac).astype(BF16)
        kf_s[n] = (kk * fac).astype(BF16)
    qe = (q * jnp.exp(b)).astype(BF16)
    kd = (kk * jnp.exp(blast - b)).astype(BF16)
    dec = jnp.exp(blast)
    vb = ai_ref[...].astype(BF16)
    for h in range(A_HEADS):
        sl = slice(h * A_HEAD, (h + 1) * A_HEAD)
        st = st_ref[h]
        o = _nt(qe[:, sl], st.astype(BF16))
        if levels:
            att = jnp.zeros((c, c), F32)
            for n in range(len(levels)):
                att = att + _nt(qf_s[n, :, sl], kf_s[n, :, sl]) * mask_s[n]
            o = o + jnp.dot(att.astype(BF16), vb[:, sl], preferred_element_type=F32)
        oi_s[:, sl] = o
        st_ref[h] = st * dec[:, sl] + _tn(vb[:, sl], kd[:, sl])

    row8 = lax.broadcasted_iota(I32, (SUBLANES, 1), 0)
    for x in range(c // SUBLANES):
        rows = slice(x * SUBLANES, (x + 1) * SUBLANES)
        outs = []
        for h in range(A_HEADS):
            sl = slice(h * A_HEAD, (h + 1) * A_HEAD)
            qh, kh, bh, vh = q_s[rows, sl], k_s[rows, sl], b_s[rows, sl], ai_ref[rows, sl]
            o = jnp.zeros((SUBLANES, A_HEAD), F32)
            for s in range(SUBLANES):
                w = jnp.exp(bh - bh[s:s + 1, :])
                a = jnp.sum(qh * (kh[s:s + 1, :] * w), axis=-1, keepdims=True)
                o = o + jnp.where(row8 >= s, a, 0.0) * vh[s:s + 1, :]
            outs.append(o)
        dg_s[rows, :] = jnp.concatenate(outs, axis=1)

    gn = gn_ref[...]
    az = az_ref[...]
    outs = []
    for h in range(A_HEADS):
        sl = slice(h * A_HEAD, (h + 1) * A_HEAD)
        o = oi_s[:, sl] + dg_s[:, sl]
        outs.append(o * lax.rsqrt(jnp.mean(o * o, axis=-1, keepdims=True) + EPS) * gn)
    o_ref[...] = jnp.concatenate(outs, axis=1) * (az * _sigmoid(az))

    @pl.when(j == pl.num_programs(1) - 1)
    def _():
        for h in range(A_HEADS):
            sout_ref[h] = st_ref[h].T


def _hgrn_chunked(u, lb_logits, gn_g, *, layer, nb, t_pad, t_valid):
    c = LANES
    nj = t_pad // c
    nlev = max(len(_hgrn_levels(c)), 1)

    def ublk(blk):
        return pl.BlockSpec((c, 512), lambda b, j: (b * nj + j, blk))

    return pl.pallas_call(
        functools.partial(_hgrn_chunk_kernel, layer=layer, c=c, t_valid=t_valid),
        grid=(nb, nj),
        in_specs=[ublk(BLK_AQ), ublk(BLK_AF), ublk(BLK_AI), ublk(BLK_AZ),
                  pl.BlockSpec(lb_logits.shape, lambda b, j: (0, 0)),
                  pl.BlockSpec((1, A_HEAD), lambda b, j: (0, 0))],
        out_specs=[pl.BlockSpec((c, A_WIDTH), lambda b, j: (b * nj + j, 0)),
                   pl.BlockSpec((None, A_HEADS, A_HEAD, A_HEAD), lambda b, j: (b, 0, 0, 0))],
        out_shape=[jax.ShapeDtypeStruct((nb * t_pad, A_WIDTH), F32),
                   jax.ShapeDtypeStruct((nb, A_HEADS, A_HEAD, A_HEAD), F32)],
        scratch_shapes=[pltpu.VMEM((A_HEADS, A_HEAD, A_HEAD), F32),
                        pltpu.VMEM((c, A_WIDTH), F32),
                        pltpu.VMEM((c, A_WIDTH), F32),
                        pltpu.VMEM((c, A_WIDTH), F32),
                        pltpu.VMEM((c, A_WIDTH), F32),
                        pltpu.VMEM((c, A_WIDTH), F32),
                        pltpu.VMEM((nlev, c, A_WIDTH), BF16),
                        pltpu.VMEM((nlev, c, A_WIDTH), BF16),
                        pltpu.VMEM((nlev, c, c), F32)],
        compiler_params=_cparams(("parallel", "arbitrary")),
        name="hgrn_chunked",
    )(u, u, u, u, lb_logits, gn_g)


def _conv_tail(y, bz, lng, lnb, pw_ref):
    mu = jnp.mean(y, axis=-1, keepdims=True)
    d = y - mu
    var = jnp.mean(d * d, axis=-1, keepdims=True)
    y = d * lax.rsqrt(var + EPS) * lng + lnb
    y = y * _sigmoid(y)
    z = jnp.dot(y.astype(BF16), pw_ref[...].astype(BF16), preferred_element_type=F32)
    return z * (bz * _sigmoid(bz))


def _conv_prompt_kernel(a_ref, g_ref, bz_ref, w_ref, cb_ref, lng_ref, lnb_ref, pw_ref, o_ref, st_ref, xb_ref, y_ref, xs_ref, *, rb, tail_off):
    j = pl.program_id(1)
    halo = 32

    @pl.when(j == 0)
    def _():
        xb_ref[0:halo, :] = jnp.zeros((halo, B_WIDTH), F32)

    xb_ref[halo:halo + rb, :] = a_ref[...] * _sigmoid(g_ref[...])
    span = rb + halo - SUBLANES
    for s in range(1, SUBLANES):
        xs_ref[s - 1, 0:span, :] = xb_ref[pl.ds(s, span), :]
    rc = rb // 2
    for c in range(B_WIDTH // LANES):
        cs = slice(c * LANES, (c + 1) * LANES)
        wc = w_ref[:, cs]
        for r in range(rb // rc):
            acc = jnp.zeros((rc, LANES), F32) + cb_ref[:, cs]
            for t in range(CONV_W):
                off = halo - (CONV_W - 1) + t
                s, base = off % SUBLANES, (off // SUBLANES) * SUBLANES + r * rc
                win = xb_ref[base:base + rc, cs] if s == 0 else xs_ref[s - 1, base:base + rc, cs]
                acc = acc + win * wc[t:t + 1, :]
            y_ref[r * rc:(r + 1) * rc, cs] = acc
    o_ref[...] = _conv_tail(y_ref[...], bz_ref[...], lng_ref[...], lnb_ref[...], pw_ref)

    @pl.when(j == pl.num_programs(1) - 1)
    def _():
        st_ref[...] = xb_ref[pl.ds(tail_off, CONV_W - 1), :]

    xb_ref[0:halo, :] = xb_ref[rb:rb + halo, :]


def _conv_prompt(u, cw, cb, lng, lnb, pw, *, nb, t_pad, t_valid):
    rb = max(t for t in range(2 * SUBLANES, 321, 2 * SUBLANES) if t_pad % t == 0)
    nj = t_pad // rb
    tail_off = (t_valid - (CONV_W - 1)) - ((nj - 1) * rb - 32)
    assert 0 <= tail_off and tail_off + CONV_W - 1 <= rb + 32

    def ublk(blk):
        return pl.BlockSpec((rb, 512), lambda b, j: (b * nj + j, blk))

    def full(a):
        return pl.BlockSpec(a.shape, lambda b, j: (0,) * a.ndim)

    return pl.pallas_call(
        functools.partial(_conv_prompt_kernel, rb=rb, tail_off=tail_off),
        grid=(nb, nj),
        in_specs=[ublk(BLK_GLUA), ublk(BLK_GLUG), ublk(BLK_BZ), full(cw), full(cb), full(lng), full(lnb), full(pw)],
        out_specs=[pl.BlockSpec((rb, B_WIDTH), lambda b, j: (b * nj + j, 0)),
                   pl.BlockSpec((None, CONV_W - 1, B_WIDTH), lambda b, j: (b, 0, 0))],
        out_shape=[jax.ShapeDtypeStruct((nb * t_pad, B_WIDTH), F32),
                   jax.ShapeDtypeStruct((nb, CONV_W - 1, B_WIDTH), F32)],
        scratch_shapes=[pltpu.VMEM((rb + 32, B_WIDTH), F32), pltpu.VMEM((rb, B_WIDTH), F32),
                        pltpu.VMEM((SUBLANES - 1, rb + 32, B_WIDTH), F32)],
        compiler_params=_cparams(("parallel", "arbitrary")),
        name="conv_prompt",
    )(u, u, u, cw, cb, lng, lnb, pw)


def _conv_decode_kernel(a_ref, g_ref, bz_ref, sin_ref, w_ref, cb_ref, lng_ref, lnb_ref, pw_ref, o_ref, sout_ref, xs_ref, *, ns, ds):
    nbuf = CONV_W - 1
    xs_ref[:, 0:nbuf, :] = sin_ref[...]
    glu = a_ref[...] * _sigmoid(g_ref[...])
    xs_ref[:, nbuf:nbuf + ds, :] = glu.reshape(ns, ds, B_WIDTH)
    acc = jnp.zeros((ns, ds, B_WIDTH), F32) + cb_ref[...]
    for t in range(CONV_W):
        acc = acc + xs_ref[:, pl.ds(t, ds), :] * w_ref[t:t + 1, :]
    o_ref[...] = _conv_tail(acc.reshape(ns * ds, B_WIDTH), bz_ref[...], lng_ref[...], lnb_ref[...], pw_ref)
    sout_ref[...] = xs_ref[:, pl.ds(ds, nbuf), :]


def _conv_decode(u, state, cw, cb, lng, lnb, pw, *, layer, nb, ds):
    ns = next(t for t in (16, 8, 4, 2, 1) if nb % t == 0)
    rb = ns * ds

    def ublk(blk):
        return pl.BlockSpec((rb, 512), lambda i: (i, blk))

    def full(a):
        return pl.BlockSpec(a.shape, lambda i: (0,) * a.ndim)

    return pl.pallas_call(
        functools.partial(_conv_decode_kernel, ns=ns, ds=ds),
        grid=(nb // ns,),
        in_specs=[ublk(BLK_GLUA), ublk(BLK_GLUG), ublk(BLK_BZ),
                  pl.BlockSpec((None, ns, CONV_W - 1, B_WIDTH), lambda i: (layer, i, 0, 0)),
                  full(cw), full(cb), full(lng), full(lnb), full(pw)],
        out_specs=[pl.BlockSpec((rb, B_WIDTH), lambda i: (i, 0)),
                   pl.BlockSpec((ns, CONV_W - 1, B_WIDTH), lambda i: (i, 0, 0))],
        out_shape=[jax.ShapeDtypeStruct((nb * ds, B_WIDTH), F32),
                   jax.ShapeDtypeStruct((nb, CONV_W - 1, B_WIDTH), F32)],
        scratch_shapes=[pltpu.VMEM((ns, CONV_W - 1 + ds, B_WIDTH), F32)],
        compiler_params=_cparams(("parallel",)),
        name="conv_decode",
    )(u, u, u, state, cw, cb, lng, lnb, pw)


def _dsa_tile_kernel(q_ref, qi_ref, cz_ref, kv_ref, kiw_ref, o_ref,
                     k16, vt16, ki16, wq_s, ik, hi16, lo16, s_buf, acc_s, *, n_sel, nblk, nb1, qw, t_pad):
    i = pl.program_id(1)
    blk = LANES
    hd = C_HEAD_DIM
    vregs = blk // SUBLANES
    qblocks = qw // blk
    groups = C_HEADS // C_KV_HEADS
    gw = groups * qw

    @pl.when(i == 0)
    def _():
        for kb in range(nblk):
            sl = slice(kb * blk, (kb + 1) * blk)
            x = kv_ref[sl, :]
            for hk in range(C_KV_HEADS):
                k16[hk * nb1 + kb] = x[:, hk * hd:(hk + 1) * hd].astype(BF16)
            vt16[kb] = x[:, blk:2 * blk].T.astype(BF16)
            ki16[kb] = kiw_ref[sl, :][:, 0:IDX_DIM].astype(BF16)
        for kb in range(nblk, nb1):
            for hk in range(C_KV_HEADS):
                k16[hk * nb1 + kb] = jnp.zeros((blk, hd), BF16)
            vt16[kb] = jnp.zeros((blk, blk), BF16)
            ki16[kb] = jnp.zeros((blk, IDX_DIM), BF16)
        wq_s[0:t_pad, :] = kiw_ref[...]
        if nb1 * blk > t_pad:
            wq_s[t_pad:nb1 * blk, :] = jnp.zeros((nb1 * blk - t_pad, blk), F32)

    npair = ((i + 1) * qblocks + 1) // 2
    r0 = pl.multiple_of(i * qw, qw)
    wq = wq_s[pl.ds(r0, qw), :]
    w_t = jnp.concatenate([wq[a * blk:(a + 1) * blk, :].T for a in range(qblocks)], axis=1)
    qi = qi_ref[...].astype(BF16)
    rowi = lax.broadcasted_iota(I32, (blk, qw), 0)
    qpos = r0 + lax.broadcasted_iota(I32, (blk, qw), 1)

    qis = jnp.concatenate([qi[:, h * IDX_DIM:(h + 1) * IDX_DIM] for h in range(IDX_HEADS)], axis=0)
    w_h = [w_t[IDX_DIM + h:IDX_DIM + h + 1, :] * IDX_SCALE for h in range(IDX_HEADS)]

    def p1(pp, c):
        for u in range(2):
            kb = 2 * pp + u
            s = _nt(ki16[kb], qis)
            acc = jnp.zeros((blk, qw), F32)
            for h in range(IDX_HEADS):
                acc = acc + jnp.maximum(s[:, h * qw:(h + 1) * qw], 0.0) * w_h[h]
            allowed = (kb * blk + rowi) <= qpos
            key = jnp.where(allowed, _sort_key(acc), jnp.int32(INT_MIN))
            ik[kb] = key
            hi16[kb] = (key >> 16).astype(I16)
        return c

    lax.fori_loop(0, npair, p1, 0)

    one16 = jnp.ones((), BF16)
    zero16 = jnp.zeros((), BF16)

    def count16(ref, pred):
        def body(pp, c):
            parts = []
            for u in range(2):
                m = jnp.where(pred(ref[2 * pp + u]), one16, zero16).reshape(vregs // 2, 2 * SUBLANES, qw)
                parts += [m[j] for j in range(vregs // 2)]
            while len(parts) > 1:
                parts = [parts[j] + parts[j + 1] for j in range(0, len(parts), 2)]
            return c + parts[0]
        c = lax.fori_loop(0, npair, body, jnp.zeros((2 * SUBLANES, qw), BF16))
        return jnp.sum(c.astype(F32), axis=0, keepdims=True)

    def search16(ref, want):
        lo = jnp.full((1, qw), -(1 << 15), I32)
        zero = jnp.zeros((1, qw), I32)
        v = jnp.where(count16(ref, lambda k: k >= zero.astype(I16)) >= want, zero, lo)

        def bit_body(t, v):
            cand = v + lax.shift_left(jnp.int32(1), jnp.int32(14) - t)
            return jnp.where(count16(ref, lambda k: k >= cand.astype(I16)) >= want, cand, v)

        return lax.fori_loop(0, 15, bit_body, v)

    thr_hi = search16(hi16, jnp.float32(n_sel))
    thr_hi16 = thr_hi.astype(I16)
    want_lo = n_sel - count16(hi16, lambda k: k > thr_hi16)

    def p2(pp, c):
        for u in range(2):
            kb = 2 * pp + u
            low = (ik[kb] & jnp.int32(0xFFFF)) - jnp.int32(1 << 15)
            lo16[kb] = jnp.where((ik[kb] >> 16) == thr_hi, low, jnp.int32(-(1 << 15))).astype(I16)
        return c

    lax.fori_loop(0, npair, p2, 0)
    thr_lo = search16(lo16, want_lo)
    thr = thr_hi * jnp.int32(1 << 16) + (thr_lo + jnp.int32(1 << 15))

    def count32(pred):
        def body(pp, c):
            for u in range(2):
                m = jnp.where(pred(ik[2 * pp + u]), 1, 0)
                c = c + jnp.sum(m.reshape(vregs, SUBLANES, qw), axis=0)
            return c
        c = lax.fori_loop(0, npair, body, jnp.zeros((SUBLANES, qw), I32))
        return jnp.sum(c, axis=0, keepdims=True)

    need = (n_sel - count32(lambda k: k > thr)).astype(F32)

    ltri = (lax.broadcasted_iota(I32, (blk, blk), 0) >= lax.broadcasted_iota(I32, (blk, blk), 1)).astype(BF16)
    q = (q_ref[...] * C_HEAD_DIM ** -0.5).astype(BF16)
    qg = [jnp.concatenate([q[:, (hk * groups + g) * hd:(hk * groups + g + 1) * hd] for g in range(groups)], axis=0)
          for hk in range(C_KV_HEADS)]

    def p3(pp, carry):
        tcar = carry[0]
        mx = list(carry[1:])
        for u in range(2):
            kb = 2 * pp + u
            key = ik[kb]
            tie = key == thr
            pre = jnp.dot(ltri, jnp.where(tie, 1.0, 0.0).astype(BF16), preferred_element_type=F32) + tcar
            allowed = (kb * blk + rowi) <= qpos
            sel = ((tie & (pre <= need)) | (key > thr)) & allowed
            bias = jnp.where(sel, 0.0, MASK_NEG)
            tcar = pre[blk - 1:blk, :]
            bias_g = jnp.concatenate([bias] * groups, axis=1)
            for hk in range(C_KV_HEADS):
                s = _nt(k16[hk * nb1 + kb], qg[hk]) + bias_g
                s_buf[hk * nb1 + kb] = s
                mx[hk] = jnp.maximum(mx[hk], jnp.max(s.reshape(vregs, SUBLANES, gw), axis=0))
        return (tcar, *mx)

    init = (jnp.zeros((1, qw), F32),) + tuple(jnp.full((SUBLANES, gw), -3.0e38, F32) for _ in range(C_KV_HEADS))
    res = lax.fori_loop(0, npair, p3, init)
    m = [jnp.max(res[1 + hk], axis=0, keepdims=True) for hk in range(C_KV_HEADS)]

    acc_s[...] = jnp.zeros((C_KV_HEADS, hd, gw), F32)

    def p4(pp, carry):
        ls = list(carry)
        for u in range(2):
            kb = 2 * pp + u
            vt = vt16[kb]
            for hk in range(C_KV_HEADS):
                p = jnp.exp(s_buf[hk * nb1 + kb] - m[hk])
                ls[hk] = ls[hk] + jnp.sum(p.reshape(vregs, SUBLANES, gw), axis=0)
                acc_s[hk] = acc_s[hk] + jnp.dot(vt[hk * hd:(hk + 1) * hd, :], p.astype(BF16), preferred_element_type=F32)
        return tuple(ls)

    ls = lax.fori_loop(0, npair, p4, tuple(jnp.zeros((SUBLANES, gw), F32) for _ in range(C_KV_HEADS)))
    outs = []
    for hk in range(C_KV_HEADS):
        og = acc_s[hk] / jnp.sum(ls[hk], axis=0, keepdims=True)
        outs += [og[:, g * qw:(g + 1) * qw] for g in range(groups)]
    o_t = jnp.concatenate(outs, axis=0)
    o = jnp.concatenate(
        [jnp.concatenate([o_t[j * blk:(j + 1) * blk, a * blk:(a + 1) * blk].T for j in range(C_WIDTH // blk)], axis=1)
         for a in range(qblocks)], axis=0)
    cz = cz_ref[...]
    o_ref[...] = o * (cz * _sigmoid(cz))


def _dsa_tiles(u, *, nb, t_pad, n_sel):
    blk = LANES
    qw = 2 * blk
    nblk = t_pad // blk
    nq = -(-t_pad // qw)
    nb1 = nq * (qw // blk)
    gw = (C_HEADS // C_KV_HEADS) * qw
    u3 = u.reshape(nb, t_pad, U_W)
    out = pl.pallas_call(
        functools.partial(_dsa_tile_kernel, n_sel=n_sel, nblk=nblk, nb1=nb1, qw=qw, t_pad=t_pad),
        grid=(nb, nq),
        in_specs=[
            pl.BlockSpec((None, qw, 512), lambda b, i: (b, i, BLK_CQ)),
            pl.BlockSpec((None, qw, 256), lambda b, i: (b, i, BLK256_CQI)),
            pl.BlockSpec((None, qw, 512), lambda b, i: (b, i, BLK_CZ)),
            pl.BlockSpec((None, t_pad, 256), lambda b, i: (b, 0, BLK256_KV)),
            pl.BlockSpec((None, t_pad, LANES), lambda b, i: (b, 0, BLK_KIW)),
        ],
        out_specs=pl.BlockSpec((None, qw, C_WIDTH), lambda b, i: (b, i, 0)),
        out_shape=jax.ShapeDtypeStruct((nb, t_pad, C_WIDTH), F32),
        scratch_shapes=[
            pltpu.VMEM((C_KV_HEADS * nb1, blk, C_HEAD_DIM), BF16),
            pltpu.VMEM((nb1, blk, blk), BF16),
            pltpu.VMEM((nb1, blk, IDX_DIM), BF16),
            pltpu.VMEM((nb1 * blk, LANES), F32),
            pltpu.VMEM((nb1, blk, qw), I32),
            pltpu.VMEM((nb1, blk, qw), I16),
            pltpu.VMEM((nb1, blk, qw), I16),
            pltpu.VMEM((C_KV_HEADS * nb1, blk, gw), F32),
            pltpu.VMEM((C_KV_HEADS, C_HEAD_DIM, gw), F32),
        ],
        compiler_params=_cparams(("parallel", "arbitrary")),
        name="dsa_tiles",
    )(u3, u3, u3, u3, u3)
    return out.reshape(nb * t_pad, C_WIDTH)


def _dsa_decode_kernel(pt_ref, q_ref, qi_ref, cz_ref, kvn_ref, kiwn_ref, ck_hbm, cv_hbm, ci_hbm, o_ref,
                       kcat, vcat, icat, kbuf, vbuf, ibuf, sems, *, layer, n_sel, npages, ds):
    b = pl.program_id(0)
    slot = b % 2
    sources = ((ck_hbm, kbuf), (cv_hbm, vbuf), (ci_hbm, ibuf))

    def page_copy(kind, page, to_slot, j):
        src, buf = sources[kind]
        return pltpu.make_async_copy(src.at[layer, page], buf.at[to_slot, j], sems.at[to_slot, kind])

    def fetch(seq, to_slot):
        for j in range(npages):
            page = pt_ref[seq * npages + j]
            for kind in range(len(sources)):
                page_copy(kind, page, to_slot, j).start()

    @pl.when(b == 0)
    def _():
        fetch(0, 0)

    for j in range(npages):
        for kind in range(len(sources)):
            page_copy(kind, 0, slot, j).wait()

    @pl.when(b + 1 < pl.num_programs(0))
    def _():
        fetch(b + 1, 1 - slot)

    kpages = [kbuf.at[slot, j] for j in range(npages)]
    vpages = [vbuf.at[slot, j] for j in range(npages)]
    ipages = [ibuf.at[slot, j] for j in range(npages)]
    q, qi, cz, kvn, kiwn = q_ref[...] * C_HEAD_DIM ** -0.5, qi_ref[...], cz_ref[...], kvn_ref[...], kiwn_ref[...]
    blk = LANES
    hd = C_HEAD_DIM
    nblk = npages + 1
    nkeys = nblk * blk
    groups = C_HEADS // C_KV_HEADS

    zpad = jnp.zeros((blk - ds, blk), F32)
    for j in range(npages):
        sl = slice(j * blk, (j + 1) * blk)
        kcat[:, sl] = kpages[j][...].astype(BF16)
        vcat[:, sl] = vpages[j][...].astype(BF16)
        icat[:, sl] = ipages[j][...].astype(BF16)
    new = slice(npages * blk, nkeys)
    kcat[:, new] = jnp.concatenate([kvn[:, 0:blk], zpad], axis=0).T.astype(BF16)
    vcat[:, new] = jnp.concatenate([kvn[:, blk:2 * blk], zpad], axis=0).T.astype(BF16)
    icat[:, new] = jnp.concatenate([kiwn, zpad], axis=0).T[0:IDX_DIM, :].astype(BF16)
    rowq = lax.broadcasted_iota(I32, (ds, nkeys), 0)
    kpos = lax.broadcasted_iota(I32, (ds, nkeys), 1)
    allowed = kpos <= npages * blk + rowq

    qis = jnp.concatenate([qi[:, h * IDX_DIM:(h + 1) * IDX_DIM] for h in range(IDX_HEADS)], axis=0).astype(BF16)
    wcol = jnp.concatenate([kiwn[:, IDX_DIM + h:IDX_DIM + h + 1] for h in range(IDX_HEADS)], axis=0) * IDX_SCALE
    r = jnp.maximum(jnp.dot(qis, icat[...], preferred_element_type=F32), 0.0) * wcol
    acc = r[0:ds]
    for h in range(1, IDX_HEADS):
        acc = acc + r[h * ds:(h + 1) * ds]
    keys = jnp.where(allowed, _sort_key(jnp.zeros((ds, nkeys), F32) + acc), jnp.int32(INT_MIN))

    def count_ge(cand):
        return jnp.sum(jnp.where(keys >= cand, 1, 0), axis=1, keepdims=True)

    thr = jnp.full((ds, 1), INT_MIN, I32)
    for step in range(8):
        shift = 28 - 4 * step
        digit = jnp.zeros((ds, 1), I32)
        for c in range(1, 16):
            inc = int(np.array(c << shift, dtype=np.uint32).astype(np.int32))
            digit = digit + jnp.where(count_ge(thr + jnp.int32(inc)) >= n_sel, 1, 0)
        thr = thr + digit * jnp.int32(1 << shift)

    gt = keys > thr
    tie = keys == thr
    need = (n_sel - jnp.sum(jnp.where(gt, 1, 0), axis=1, keepdims=True)).astype(F32)
    tief = jnp.where(tie, 1.0, 0.0)
    ri = lax.broadcasted_iota(I32, (blk, blk), 0)
    ci = lax.broadcasted_iota(I32, (blk, blk), 1)
    utri = (ri <= ci).astype(F32)
    carry = jnp.zeros((ds, 1), F32)
    biases = []
    for j in range(nblk):
        sl = slice(j * blk, (j + 1) * blk)
        pre = jnp.dot(tief[:, sl], utri, preferred_element_type=F32) + carry
        sel = ((tie[:, sl] & (pre <= need)) | gt[:, sl]) & allowed[:, sl]
        biases.append(jnp.where(sel, 0.0, MASK_NEG))
        carry = carry + jnp.sum(tief[:, sl], axis=1, keepdims=True)
    bias = jnp.concatenate(biases, axis=1)
    bias_all = jnp.concatenate([bias] * C_HEADS, axis=0)

    zq = jnp.zeros((groups * ds, hd), F32)
    qrows = []
    for hk in range(C_KV_HEADS):
        qg = jnp.concatenate([q[:, (hk * groups + g) * hd:(hk * groups + g + 1) * hd] for g in range(groups)], axis=0)
        qrows.append(jnp.concatenate([qg if c == hk else zq for c in range(C_KV_HEADS)], axis=1))
    qd = jnp.concatenate(qrows, axis=0).astype(BF16)
    s = jnp.dot(qd, kcat[...], preferred_element_type=F32) + bias_all
    m = jnp.max(s, axis=1, keepdims=True)
    p = jnp.exp(s - m)
    l = jnp.sum(p, axis=1, keepdims=True)
    og = _nt(p.astype(BF16), vcat[...]) / l
    outs = []
    for hk in range(C_KV_HEADS):
        for g in range(groups):
            r0 = (hk * groups + g) * ds
            outs.append(og[r0:r0 + ds, hk * hd:(hk + 1) * hd])
    o = jnp.concatenate(outs, axis=1)
    o_ref[...] = o * (cz * _sigmoid(cz))


def _dsa_decode(u, cache_k, cache_v, cache_kidx, page_table, *, layer, nb, ds, n_sel):
    npages = page_table.shape[1]
    pt = page_table.reshape(-1)
    kvw = C_KV_HEADS * C_HEAD_DIM
    hbm = pl.BlockSpec(memory_space=pl.ANY)
    grid_spec = pltpu.PrefetchScalarGridSpec(
        num_scalar_prefetch=1,
        grid=(nb,),
        in_specs=[
            pl.BlockSpec((ds, 512), lambda b, pt_ref: (b, BLK_CQ)),
            pl.BlockSpec((ds, 256), lambda b, pt_ref: (b, BLK256_CQI)),
            pl.BlockSpec((ds, 512), lambda b, pt_ref: (b, BLK_CZ)),
            pl.BlockSpec((ds, 256), lambda b, pt_ref: (b, BLK256_KV)),
            pl.BlockSpec((ds, LANES), lambda b, pt_ref: (b, BLK_KIW)),
            hbm, hbm, hbm,
        ],
        out_specs=pl.BlockSpec((ds, C_WIDTH), lambda b, pt_ref: (b, 0)),
        scratch_shapes=[pltpu.VMEM((kvw, (npages + 1) * LANES), BF16),
                        pltpu.VMEM((kvw, (npages + 1) * LANES), BF16),
                        pltpu.VMEM((IDX_DIM, (npages + 1) * LANES), BF16),
                        pltpu.VMEM((2, npages, kvw, PAGE_SIZE), F32),
                        pltpu.VMEM((2, npages, kvw, PAGE_SIZE), F32),
                        pltpu.VMEM((2, npages, IDX_DIM, PAGE_SIZE), F32),
                        pltpu.SemaphoreType.DMA((2, 3))],
    )
    return pl.pallas_call(
        functools.partial(_dsa_decode_kernel, layer=layer, n_sel=n_sel, npages=npages, ds=ds),
        grid_spec=grid_spec,
        out_shape=jax.ShapeDtypeStruct((nb * ds, C_WIDTH), F32),
        compiler_params=_cparams(("arbitrary",)),
        name="dsa_decode",
    )(pt, u, u, u, u, u, cache_k, cache_v, cache_kidx)


def _merge_kernel(h_ref, ya_ref, yb_ref, yc_ref, ga_ref, gb_ref, gc_ref, wpa_ref, wpb_ref, wpc_ref, wo_ref, fg_ref, o_ref, *, final):
    def proj(y_ref, w_ref):
        return jnp.dot(y_ref[...].astype(BF16), w_ref[...].astype(BF16), preferred_element_type=F32)

    m = (_sigmoid(ga_ref[...]) * proj(ya_ref, wpa_ref)
         + _sigmoid(gb_ref[...]) * proj(yb_ref, wpb_ref)
         + _sigmoid(gc_ref[...]) * proj(yc_ref, wpc_ref))
    h = h_ref[...] + jnp.dot(m.astype(BF16), wo_ref[...].astype(BF16), preferred_element_type=F32)
    if final:
        h = h * lax.rsqrt(jnp.mean(h * h, axis=-1, keepdims=True) + EPS) * fg_ref[...]
    o_ref[...] = h


def _merge(h, ya, yb, yc, u, wpa, wpb, wpc, wo, fg, *, final):
    rows = h.shape[0]
    tm = next(t for t in (256, 128, 64, 32, 16, 8) if rows % t == 0)

    def rowblk(width, blk=0):
        return pl.BlockSpec((tm, width), lambda i: (i, blk))

    def full(a):
        return pl.BlockSpec(a.shape, lambda i: (0,) * a.ndim)

    return pl.pallas_call(
        functools.partial(_merge_kernel, final=final),
        grid=(rows // tm,),
        in_specs=[rowblk(D_MODEL), rowblk(512), rowblk(512), rowblk(512),
                  rowblk(D_MODEL, BLK_GATE), rowblk(D_MODEL, BLK_GATE + 1), rowblk(D_MODEL, BLK_GATE + 2),
                  full(wpa), full(wpb), full(wpc), full(wo), full(fg)],
        out_specs=rowblk(D_MODEL),
        out_shape=jax.ShapeDtypeStruct((rows, D_MODEL), F32),
        compiler_params=_cparams(("parallel",)),
        name="merge",
    )(h, ya, yb, yc, u, u, u, wpa, wpb, wpc, wo, fg)


def _rope_tables(pos):
    half = ROT_DIM // 2
    inv = ROPE_THETA ** (-jnp.arange(half, dtype=F32) * 2.0 / ROT_DIM)
    ang = pos.astype(F32)[:, None] * jnp.tile(inv, LANES // half)[None, :]
    return jnp.cos(ang), jnp.sin(ang)


def _cache_rows(u, nb, t_pad, t_valid):
    u3 = u.reshape(nb, t_pad, U_W)[:, :t_valid]
    k0, ki0 = BLK256_KV * 256, BLK_KIW * LANES
    heads = (nb, t_valid, C_KV_HEADS, C_HEAD_DIM)
    return (u3[..., k0:k0 + LANES].reshape(heads), u3[..., k0 + LANES:k0 + 2 * LANES].reshape(heads),
            u3[..., ki0:ki0 + IDX_DIM])


def _pack_cols(w):
    split = 4608
    tail = split + 68
    pad = jnp.zeros(w.shape[:-1] + (U_W - N_IN,), w.dtype)
    return jnp.concatenate([w[..., :split], w[..., tail:], w[..., split:tail], pad], axis=-1)


def kernel(x_prompt, x_sample, cache_k, cache_v, cache_kidx, state_hgrn, state_conv, page_table, meta_tokens, norm_g, w_in, b_in, lb_logits, hgrn_norm_g, conv_w, conv_b, conv_ln_g, conv_ln_b, conv_pw, w_pa, w_pb, w_pc, w_out, final_norm_g):
    nbp, seq, _ = x_prompt.shape
    nbs, ds, _ = x_sample.shape
    depth = w_in.shape[0]
    npages = page_table.shape[1]
    past = npages * PAGE_SIZE
    t_valid = seq + N_META
    t_pad = -(-t_valid // LANES) * LANES
    n_sel_p = min(TOPK_MAX, t_valid // 4)
    n_sel_s = min(TOPK_MAX, (past + ds) // 4)
    n_phys = cache_k.shape[1]

    meta = jnp.broadcast_to(meta_tokens[None].astype(F32), (nbp, N_META, D_MODEL))
    hp = jnp.concatenate([meta, x_prompt, jnp.zeros((nbp, t_pad - t_valid, D_MODEL), F32)], axis=1).reshape(nbp * t_pad, D_MODEL)
    hs = x_sample.reshape(nbs * ds, D_MODEL)
    cos_p, sin_p = _rope_tables(jnp.tile(jnp.arange(t_pad), nbp))
    cos_s, sin_s = _rope_tables(jnp.tile(past + jnp.arange(ds), nbs))

    w_in_p = _pack_cols(w_in)
    b_in_p = _pack_cols(b_in).reshape(depth, 1, U_W)
    ck4 = cache_k.transpose(0, 1, 3, 4, 2).reshape(depth, n_phys, C_KV_HEADS * C_HEAD_DIM, PAGE_SIZE)
    cv4 = cache_v.transpose(0, 1, 3, 4, 2).reshape(depth, n_phys, C_KV_HEADS * C_HEAD_DIM, PAGE_SIZE)
    ci4 = cache_kidx.transpose(0, 1, 3, 2)
    fg = final_norm_g.reshape(1, D_MODEL)

    outs = {k: [] for k in ("pk", "pv", "pki", "ps", "pc", "sk", "sv", "ski", "ss", "sc")}
    for l in range(depth):
        g = norm_g[l].reshape(1, D_MODEL)
        gn = hgrn_norm_g[l].reshape(1, A_HEAD)
        cw, cb = conv_w[l], conv_b[l].reshape(1, B_WIDTH)
        lng, lnb = conv_ln_g[l].reshape(1, B_WIDTH), conv_ln_b[l].reshape(1, B_WIDTH)
        pw = conv_pw[l]
        wpa, wpb, wpc, wo = (w[l] for w in (w_pa, w_pb, w_pc, w_out))
        final = l == depth - 1

        u = _inproj(hp, g, w_in_p[l], b_in_p[l], cos_p, sin_p)
        ya, s_new = _hgrn_chunked(u, lb_logits, gn, layer=l, nb=nbp, t_pad=t_pad, t_valid=t_valid)
        yb, c_new = _conv_prompt(u, cw, cb, lng, lnb, pw, nb=nbp, t_pad=t_pad, t_valid=t_valid)
        yc = _dsa_tiles(u, nb=nbp, t_pad=t_pad, n_sel=n_sel_p)
        hp = _merge(hp, ya, yb, yc, u, wpa, wpb, wpc, wo, fg, final=final)
        k_new, v_new, ki_new = _cache_rows(u, nbp, t_pad, t_valid)
        for name, val in zip(("pk", "pv", "pki", "ps", "pc"), (k_new, v_new, ki_new, s_new, c_new)):
            outs[name].append(val)

        u = _inproj(hs, g, w_in_p[l], b_in_p[l], cos_s, sin_s)
        ya, s_new = _hgrn(u, lb_logits, gn, state_hgrn, layer=l, nb=nbs, t_pad=ds, t_valid=ds, rb=ds, sb=ds, state_layer=l)
        yb, c_new = _conv_decode(u, state_conv, cw, cb, lng, lnb, pw, layer=l, nb=nbs, ds=ds)
        yc = _dsa_decode(u, ck4, cv4, ci4, page_table, layer=l, nb=nbs, ds=ds, n_sel=n_sel_s)
        hs = _merge(hs, ya, yb, yc, u, wpa, wpb, wpc, wo, fg, final=final)
        k_new, v_new, ki_new = _cache_rows(u, nbs, ds, ds)
        for name, val in zip(("sk", "sv", "ski", "ss", "sc"), (k_new, v_new, ki_new, s_new, c_new)):
            outs[name].append(val)

    y_prompt = hp.reshape(nbp, t_pad, D_MODEL)[:, N_META:t_valid]
    y_sample = hs.reshape(nbs, ds, D_MODEL)
    st = {k: jnp.stack(v) for k, v in outs.items()}
    return (y_prompt, y_sample, st["pk"], st["pv"], st["pki"], st["ps"], st["pc"],
            st["sk"], st["sv"], st["ski"], st["ss"], st["sc"])
```
